```python
import functools
import jax
import jax.numpy as jnp
from jax import lax
import numpy as np

D_MODEL = 1024
BATCH = 8
SEQ = 4096
DEPTH = 2

GRID_W = 64
CTX_LEN = 256
N_EVEN = (DEPTH + 1) // 2
N_ODD = DEPTH // 2
N_MOD = 6
EPS = 1e-6
ROPE_THETA = 10000.0
BLOCK = 128
NEG_INF = -1e30

CONV_A_DIM = D_MODEL // 2
CONV_A_WIDTH = 31
WIN_HEADS = 8
WIN_KV_HEADS = 2
WIN_HEAD_DIM = 64
WINDOW = 128
LRU_DIM = D_MODEL // 2
LRU_BLOCKS = 8
LRU_BLOCK_DIM = LRU_DIM // LRU_BLOCKS
LRU_CONV_WIDTH = 4
LRU_C = 8.0
MLA_HEADS = 8
MLA_Q_RANK = 384
MLA_KV_RANK = 256
MLA_NOPE = 64
MLA_ROPE = 32
MLA_V = 64
FFN_DIM = 2816
FFN_CONV_WIDTH = 3

AB_IN = 2 * CONV_A_DIM + (WIN_HEADS + 2 * WIN_KV_HEADS) * WIN_HEAD_DIM
AB_OUT = CONV_A_DIM + WIN_HEADS * WIN_HEAD_DIM
CD_IN = 2 * LRU_DIM + MLA_Q_RANK + MLA_KV_RANK + MLA_ROPE
CD_OUT = LRU_DIM + MLA_HEADS * MLA_V

kernel_name = "hybrid_prefix_dit_block"


def rms_norm(x, g):
    x32 = x.astype(jnp.float32)
    y = x32 * lax.rsqrt(jnp.mean(x32 * x32, axis=-1, keepdims=True) + EPS)
    return (y * g.astype(jnp.float32)).astype(x.dtype)


def layer_norm(x, g, b):
    x32 = x.astype(jnp.float32)
    mu = jnp.mean(x32, axis=-1, keepdims=True)
    var = jnp.mean(jnp.square(x32 - mu), axis=-1, keepdims=True)
    y = (x32 - mu) * lax.rsqrt(var + EPS)
    return (y * g.astype(jnp.float32) + b.astype(jnp.float32)).astype(x.dtype)


def modulate(h, shift, scale):
    return h * (1 + scale) + shift


def dwconv(x, w, b, pad_left, pad_right):
    y = lax.conv_general_dilated(x, w[:, None, :].astype(x.dtype), window_strides=(1,),
                                 padding=[(pad_left, pad_right)],
                                 dimension_numbers=('NWC', 'WIO', 'NWC'),
                                 feature_group_count=x.shape[-1])
    return y + b.astype(x.dtype)


def axial_rope_tables(rows, dim):
    row = jnp.repeat(jnp.arange(rows), GRID_W).astype(jnp.float32)
    col = jnp.tile(jnp.arange(GRID_W), rows).astype(jnp.float32)
    nf = dim // 4
    inv = ROPE_THETA ** (-jnp.arange(nf, dtype=jnp.float32) / nf)
    ar = row[:, None] * inv
    ac = col[:, None] * inv
    ang = jnp.concatenate([ar, ar, ac, ac], axis=-1)
    return jnp.cos(ang), jnp.sin(ang)


def apply_rope(x, cos, sin):
    q = x.shape[-1] // 4
    x1, x2, x3, x4 = x[..., :q], x[..., q:2 * q], x[..., 2 * q:3 * q], x[..., 3 * q:]
    partner = jnp.concatenate([-x2, x1, -x4, x3], axis=-1)
    shape = (cos.shape[0],) + (1,) * (x.ndim - 3) + (cos.shape[1],)
    return (x * cos.reshape(shape) + partner * sin.reshape(shape)).astype(x.dtype)


def joint_softmax(logits_list):
    m = functools.reduce(jnp.maximum, [jnp.max(l, axis=-1, keepdims=True) for l in logits_list])
    e = [jnp.exp(l - m) for l in logits_list]
    denom = functools.reduce(jnp.add, [jnp.sum(z, axis=-1, keepdims=True) for z in e])
    return [z / denom for z in e]


def conformer_conv(z, conv_w, conv_b, ln_g, ln_b):
    val, gate = z[..., :CONV_A_DIM], z[..., CONV_A_DIM:]
    u = val * jax.nn.sigmoid(gate)
    half = CONV_A_WIDTH // 2
    u = dwconv(u, conv_w, conv_b, half, half)
    u = layer_norm(u, ln_g, ln_b)
    return jax.nn.silu(u)


def window_gqa_latent(q, q_raw, k, v, kc, vc, sink):
    b, t, h, d = q.shape
    kvh = k.shape[2]
    g = h // kvh
    nb = t // BLOCK
    scale = d ** -0.5
    qb = q.reshape(b, nb, BLOCK, kvh, g, d)
    qrb = q_raw.reshape(b, nb, BLOCK, kvh, g, d)

    def band(z):
        zp = jnp.pad(z, ((0, 0), (BLOCK, BLOCK), (0, 0), (0, 0))).reshape(b, nb + 2, BLOCK, kvh, d)
        return jnp.concatenate([zp[:, :-2], zp[:, 1:-1], zp[:, 2:]], axis=2)

    kb, vb = band(k), band(v)
    s_lat = jnp.einsum('bnqkgd,bnskd->bnkgqs', qb, kb, preferred_element_type=jnp.float32) * scale
    s_ctx = jnp.einsum('bnqkgd,bckd->bnkgqc', qrb, kc, preferred_element_type=jnp.float32) * scale
    qpos = jnp.arange(nb)[:, None, None] * BLOCK + jnp.arange(BLOCK)[None, :, None]
    kpos = jnp.arange(nb)[:, None, None] * BLOCK - BLOCK + jnp.arange(3 * BLOCK)[None, None, :]
    valid = (jnp.abs(qpos - kpos) <= WINDOW) & (kpos >= 0) & (kpos < t)
    s_lat = jnp.where(valid[None, :, None, None], s_lat, NEG_INF)
    sink_l = sink.astype(jnp.float32).reshape(1, 1, kvh, g, 1, 1)
    p_lat, p_ctx, _ = joint_softmax([s_lat, s_ctx, sink_l])
    out = (jnp.einsum('bnkgqs,bnskd->bnqkgd', p_lat.astype(v.dtype), vb)
           + jnp.einsum('bnkgqc,bckd->bnqkgd', p_ctx.astype(vc.dtype), vc))
    return out.reshape(b, t, h * d)


def context_gqa(qc, kc, vc, sink):
    b, l, h, d = qc.shape
    kvh = kc.shape[2]
    g = h // kvh
    qg = qc.reshape(b, l, kvh, g, d)
    s = jnp.einsum('bqkgd,bskd->bkgqs', qg, kc, preferred_element_type=jnp.float32) * d ** -0.5
    p, _ = joint_softmax([s, sink.astype(jnp.float32).reshape(1, kvh, g, 1, 1)])
    out = jnp.einsum('bkgqs,bskd->bqkgd', p.astype(vc.dtype), vc)
    return out.reshape(b, l, h * d)


def mixer_ab(hl, hc, w_in, conv_w, conv_b, ln_g, ln_b, sink, w_out, rope, need_ctx):
    cos, sin = rope
    n_glu = 2 * CONV_A_DIM
    n_q = WIN_HEADS * WIN_HEAD_DIM
    n_kv = WIN_KV_HEADS * WIN_HEAD_DIM

    def split(z):
        lead = z.shape[:-1]
        glu = z[..., :n_glu]
        q = z[..., n_glu:n_glu + n_q].reshape(*lead, WIN_HEADS, WIN_HEAD_DIM)
        k = z[..., n_glu + n_q:n_glu + n_q + n_kv].reshape(*lead, WIN_KV_HEADS, WIN_HEAD_DIM)
        v = z[..., n_glu + n_q + n_kv:].reshape(*lead, WIN_KV_HEADS, WIN_HEAD_DIM)
        return glu, q, k, v

    glu_l, q_l, k_l, v_l = split(hl @ w_in)
    glu_c, q_c, k_c, v_c = split(hc @ w_in)
    a_l = conformer_conv(glu_l, conv_w, conv_b, ln_g, ln_b)
    b_l = window_gqa_latent(apply_rope(q_l, cos, sin), q_l, apply_rope(k_l, cos, sin), v_l, k_c, v_c, sink)
    y_l = jnp.concatenate([a_l, b_l], axis=-1) @ w_out
    if not need_ctx:
        return y_l, None
    a_c = conformer_conv(glu_c, conv_w, conv_b, ln_g, ln_b)
    b_c = context_gqa(q_c, k_c, v_c, sink)
    y_c = jnp.concatenate([a_c, b_c], axis=-1) @ w_out
    return y_l, y_c


def block_diag(x, w, b):
    xr = x.reshape(*x.shape[:-1], LRU_BLOCKS, LRU_BLOCK_DIM)
    return jnp.einsum('btnh,nhk->btnk', xr, w).reshape(x.shape) + b


def linear_scan(a, bx, h0, reverse):
    def combine(left, right):
        a1, b1 = left
        a2, b2 = right
        return a1 * a2, a2 * b1 + b2
    a_cum, b_cum = lax.associative_scan(combine, (a, bx), axis=1, reverse=reverse)
    return a_cum * h0[:, None, :] + b_cum


def rglru_direction(u_lat, u_ctx, conv_w, conv_b, gate_w, gate_b, lam, reverse):
    pad = (0, LRU_CONV_WIDTH - 1) if reverse else (LRU_CONV_WIDTH - 1, 0)
    neg_sp = jax.nn.softplus(-lam.astype(jnp.float32))

    def coeffs(u):
        xc = dwconv(u, conv_w, conv_b, pad[0], pad[1])
        r = jax.nn.sigmoid(block_diag(xc, gate_w[0], gate_b[0]).astype(jnp.float32))
        i = jax.nn.sigmoid(block_diag(xc, gate_w[1], gate_b[1]).astype(jnp.float32))
        log_a = -LRU_C * r * neg_sp
        bx = jnp.sqrt(-jnp.expm1(2.0 * log_a)) * i * xc.astype(jnp.float32)
        return jnp.exp(log_a), bx

    a_c, b_c = coeffs(u_ctx)
    h_ctx = linear_scan(a_c, b_c, jnp.zeros((u_ctx.shape[0], u_ctx.shape[-1]), jnp.float32), reverse)
    h0 = h_ctx[:, 0] if reverse else h_ctx[:, -1]
    a_l, b_l = coeffs(u_lat)
    h_lat = linear_scan(a_l, b_l, h0, reverse)
    return h_lat, h_ctx


def mla_q(cq, q_norm, w_uq):
    q = (rms_norm(cq, q_norm) @ w_uq).reshape(*cq.shape[:-1], MLA_HEADS, MLA_NOPE + MLA_ROPE)
    return q[..., :MLA_NOPE], q[..., MLA_NOPE:]


def mla_kv(ckv, kv_norm, w_ukv):
    kv = (rms_norm(ckv, kv_norm) @ w_ukv).reshape(*ckv.shape[:-1], MLA_HEADS, MLA_NOPE + MLA_V)
    return kv[..., :MLA_NOPE], kv[..., MLA_NOPE:]


def mla_latent(q_nope, q_rope, q_rope_raw, k_nope, k_rope, v, kc_nope, kc_rope, vc):
    b, t, h, _ = q_nope.shape
    nb = t // BLOCK
    scale = (MLA_NOPE + MLA_ROPE) ** -0.5

    def to_blocks(z):
        return jnp.moveaxis(z.reshape(b, nb, BLOCK, *z.shape[2:]), 1, 0)

    def one_block(args):
        qn, qr, qrr = args
        s_lat = (jnp.einsum('bqhd,bshd->bhqs', qn, k_nope, preferred_element_type=jnp.float32)
                 + jnp.einsum('bqhr,bsr->bhqs', qr, k_rope, preferred_element_type=jnp.float32)) * scale
        s_ctx = (jnp.einsum('bqhd,bshd->bhqs', qn, kc_nope, preferred_element_type=jnp.float32)
                 + jnp.einsum('bqhr,bsr->bhqs', qrr, kc_rope, preferred_element_type=jnp.float32)) * scale
        p_lat, p_ctx = joint_softmax([s_lat, s_ctx])
        return (jnp.einsum('bhqs,bshd->bqhd', p_lat.astype(v.dtype), v)
                + jnp.einsum('bhqs,bshd->bqhd', p_ctx.astype(vc.dtype), vc))

    out = lax.map(one_block, (to_blocks(q_nope), to_blocks(q_rope), to_blocks(q_rope_raw)))
    return jnp.moveaxis(out, 0, 1).reshape(b, t, h * MLA_V)


def mla_context(qn, qr, kn, kr, v):
    scale = (MLA_NOPE + MLA_ROPE) ** -0.5
    s = (jnp.einsum('bqhd,bshd->bhqs', qn, kn, preferred_element_type=jnp.float32)
         + jnp.einsum('bqhr,bsr->bhqs', qr, kr, preferred_element_type=jnp.float32)) * scale
    p = jax.nn.softmax(s, axis=-1)
    out = jnp.einsum('bhqs,bshd->bqhd', p.astype(v.dtype), v)
    return out.reshape(*qn.shape[:2], MLA_HEADS * MLA_V)


def mixer_cd(hl, hc, w_in, lru_conv_w, lru_conv_b, lru_gate_w, lru_gate_b, lru_lambda,
             q_norm, w_uq, kv_norm, w_ukv, w_out, rope, need_ctx):
    cos, sin = rope
    o1 = LRU_DIM
    o2 = 2 * LRU_DIM
    o3 = o2 + MLA_Q_RANK
    o4 = o3 + MLA_KV_RANK

    def split(z):
        return z[..., :o1], z[..., o1:o2], z[..., o2:o3], z[..., o3:o4], z[..., o4:]

    xb_l, gt_l, cq_l, ckv_l, kr_l = split(hl @ w_in)
    xb_c, gt_c, cq_c, ckv_c, kr_c = split(hc @ w_in)
    hf_l, hf_c = rglru_direction(xb_l, xb_c, lru_conv_w[0], lru_conv_b[0], lru_gate_w[0], lru_gate_b[0],
                                 lru_lambda[0], False)
    hb_l, hb_c = rglru_direction(xb_l, xb_c, lru_conv_w[1], lru_conv_b[1], lru_gate_w[1], lru_gate_b[1],
                                 lru_lambda[1], True)
    c_l = ((hf_l + hb_l) * jax.nn.gelu(gt_l.astype(jnp.float32))).astype(hl.dtype)
    qn_l, qr_l = mla_q(cq_l, q_norm, w_uq)
    kn_l, v_l = mla_kv(ckv_l, kv_norm, w_ukv)
    kn_c, v_c = mla_kv(ckv_c, kv_norm, w_ukv)
    d_l = mla_latent(qn_l, apply_rope(qr_l, cos, sin), qr_l, kn_l, apply_rope(kr_l, cos, sin), v_l,
                     kn_c, kr_c, v_c)
    y_l = jnp.concatenate([c_l, d_l], axis=-1) @ w_out
    if not need_ctx:
        return y_l, None
    c_c = ((hf_c + hb_c) * jax.nn.gelu(gt_c.astype(jnp.float32))).astype(hc.dtype)
    qn_c, qr_c = mla_q(cq_c, q_norm, w_uq)
    d_c = mla_context(qn_c, qr_c, kn_c, kr_c, v_c)
    y_c = jnp.concatenate([c_c, d_c], axis=-1) @ w_out
    return y_l, y_c


def conv_glu_ffn(h, w_up, conv_w, conv_b, w_down):
    z = h @ w_up
    g, u = z[..., :FFN_DIM], z[..., FFN_DIM:]
    half = FFN_CONV_WIDTH // 2
    g = dwconv(g, conv_w, conv_b, half, half)
    return (jax.nn.gelu(g) * u) @ w_down


def setup_inputs(seed: int = 0) -> dict:
    key = jax.random.key(seed)
    ks = iter(jax.random.split(key, 40))
    f32 = jnp.float32

    def nrm(shape, fan_in, scale=1.0):
        return jax.random.normal(next(ks), shape, f32) * (scale * fan_in ** -0.5)

    def gain(shape):
        return 1.0 + 0.05 * jax.random.normal(next(ks), shape, f32)

    def small(shape):
        return 0.01 * jax.random.normal(next(ks), shape, f32)

    x = jax.random.normal(next(ks), (BATCH, SEQ, D_MODEL), f32)
    c = jax.random.normal(next(ks), (BATCH, D_MODEL), f32)
    ctx = jax.random.normal(next(ks), (BATCH, CTX_LEN, D_MODEL), f32)
    c_ctx = jax.random.normal(next(ks), (D_MODEL,), f32)
    w_mod = nrm((DEPTH, D_MODEL, N_MOD * D_MODEL), D_MODEL, 0.5)
    b_mod = small((DEPTH, N_MOD * D_MODEL))
    norm_g = gain((DEPTH, 4, D_MODEL))
    ffn_w_up = nrm((DEPTH, D_MODEL, 2 * FFN_DIM), D_MODEL)
    ffn_conv_w = nrm((DEPTH, FFN_CONV_WIDTH, FFN_DIM), FFN_CONV_WIDTH)
    ffn_conv_b = small((DEPTH, FFN_DIM))
    ffn_w_down = nrm((DEPTH, FFN_DIM, D_MODEL), FFN_DIM)
    ab_w_in = nrm((N_EVEN, D_MODEL, AB_IN), D_MODEL)
    a_conv_w = nrm((N_EVEN, CONV_A_WIDTH, CONV_A_DIM), CONV_A_WIDTH)
    a_conv_b = small((N_EVEN, CONV_A_DIM))
    a_ln_g = gain((N_EVEN, CONV_A_DIM))
    a_ln_b = small((N_EVEN, CONV_A_DIM))
    b_sink = 0.5 * jax.random.normal(next(ks), (N_EVEN, WIN_HEADS), f32)
    ab_w_out = nrm((N_EVEN, AB_OUT, D_MODEL), AB_OUT)
    cd_w_in = nrm((N_ODD, D_MODEL, CD_IN), D_MODEL)
    lru_conv_w = nrm((N_ODD, 2, LRU_CONV_WIDTH, LRU_DIM), LRU_CONV_WIDTH)
    lru_conv_b = small((N_ODD, 2, LRU_DIM))
    lru_gate_w = nrm((N_ODD, 2, 2, LRU_BLOCKS, LRU_BLOCK_DIM, LRU_BLOCK_DIM), LRU_BLOCK_DIM)
    lru_gate_b = small((N_ODD, 2, 2, LRU_DIM))
    a_pow_c = jax.random.uniform(next(ks), (N_ODD, 2, LRU_DIM), f32, 0.9, 0.999)
    a0 = a_pow_c ** (1.0 / LRU_C)
    lru_lambda = jnp.log(a0) - jnp.log1p(-a0)
    mla_q_norm = gain((N_ODD, MLA_Q_RANK))
    mla_w_uq = nrm((N_ODD, MLA_Q_RANK, MLA_HEADS * (MLA_NOPE + MLA_ROPE)), MLA_Q_RANK)
    mla_kv_norm = gain((N_ODD, MLA_KV_RANK))
    mla_w_ukv = nrm((N_ODD, MLA_KV_RANK, MLA_HEADS * (MLA_NOPE + MLA_V)), MLA_KV_RANK)
    cd_w_out = nrm((N_ODD, CD_OUT, D_MODEL), CD_OUT)
    return {"x": x, "c": c, "ctx": ctx, "c_ctx": c_ctx, "w_mod": w_mod, "b_mod": b_mod,
            "norm_g": norm_g, "ffn_w_up": ffn_w_up, "ffn_conv_w": ffn_conv_w, "ffn_conv_b": ffn_conv_b,
            "ffn_w_down": ffn_w_down, "ab_w_in": ab_w_in, "a_conv_w": a_conv_w, "a_conv_b": a_conv_b,
            "a_ln_g": a_ln_g, "a_ln_b": a_ln_b, "b_sink": b_sink, "ab_w_out": ab_w_out,
            "cd_w_in": cd_w_in, "lru_conv_w": lru_conv_w, "lru_conv_b": lru_conv_b,
            "lru_gate_w": lru_gate_w, "lru_gate_b": lru_gate_b, "lru_lambda": lru_lambda,
            "mla_q_norm": mla_q_norm, "mla_w_uq": mla_w_uq, "mla_kv_norm": mla_kv_norm,
            "mla_w_ukv": mla_w_ukv, "cd_w_out": cd_w_out}


def reference(x, c, ctx, c_ctx, w_mod, b_mod, norm_g, ffn_w_up, ffn_conv_w, ffn_conv_b, ffn_w_down,
              ab_w_in, a_conv_w, a_conv_b, a_ln_g, a_ln_b, b_sink, ab_w_out,
              cd_w_in, lru_conv_w, lru_conv_b, lru_gate_w, lru_gate_b, lru_lambda,
              mla_q_norm, mla_w_uq, mla_kv_norm, mla_w_ukv, cd_w_out):
    n_tokens = x.shape[1]
    rows = n_tokens // GRID_W
    rope_win = axial_rope_tables(rows, WIN_HEAD_DIM)
    rope_mla = axial_rope_tables(rows, MLA_ROPE)
    xl, xc = x, ctx
    for layer in range(DEPTH):
        last = layer == DEPTH - 1
        j = layer // 2
        ml = (jax.nn.silu(c) @ w_mod[layer] + b_mod[layer]).reshape(c.shape[0], 1, N_MOD, D_MODEL)
        mc = (jax.nn.silu(c_ctx) @ w_mod[layer] + b_mod[layer]).reshape(N_MOD, D_MODEL)
        g = norm_g[layer]
        hl = modulate(rms_norm(xl, g[0]), ml[:, :, 0], ml[:, :, 1])
        hc = modulate(rms_norm(xc, g[0]), mc[0], mc[1])
        if layer % 2 == 0:
            yl, yc = mixer_ab(hl, hc, ab_w_in[j], a_conv_w[j], a_conv_b[j], a_ln_g[j], a_ln_b[j],
                              b_sink[j], ab_w_out[j], rope_win, not last)
        else:
            yl, yc = mixer_cd(hl, hc, cd_w_in[j], lru_conv_w[j], lru_conv_b[j], lru_gate_w[j],
                              lru_gate_b[j], lru_lambda[j], mla_q_norm[j], mla_w_uq[j], mla_kv_norm[j],
                              mla_w_ukv[j], cd_w_out[j], rope_mla, not last)
        xl = xl + ml[:, :, 2] * rms_norm(yl, g[1])
        hl = modulate(rms_norm(xl, g[2]), ml[:, :, 3], ml[:, :, 4])
        xl = xl + ml[:, :, 5] * rms_norm(
            conv_glu_ffn(hl, ffn_w_up[layer], ffn_conv_w[layer], ffn_conv_b[layer], ffn_w_down[layer]), g[3])
        if not last:
            xc = xc + mc[2] * rms_norm(yc, g[1])
            hc = modulate(rms_norm(xc, g[2]), mc[3], mc[4])
            xc = xc + mc[5] * rms_norm(
                conv_glu_ffn(hc, ffn_w_up[layer], ffn_conv_w[layer], ffn_conv_b[layer], ffn_w_down[layer]), g[3])
    return xl
```

```python
import functools

import jax
import jax.numpy as jnp
from jax import lax
from jax.experimental import pallas as pl
from jax.experimental.pallas import tpu as pltpu

_BF = jnp.bfloat16
_F32 = jnp.float32

D_MODEL = 1024
N_MOD = 6
EPS = 1e-6
ROPE_THETA = 10000.0
GRID_W = 64
NEG_INF = -1e30

CONV_A_DIM = 512
CONV_A_WIDTH = 31
WIN_HEADS = 8
WIN_KV_HEADS = 2
WIN_HEAD_DIM = 64
WINDOW = 128
LRU_DIM = 512
LRU_BLOCKS = 8
LRU_CONV_WIDTH = 4
LRU_C = 8.0
MLA_HEADS = 8
MLA_Q_RANK = 384
MLA_KV_RANK = 256
MLA_NOPE = 64
MLA_ROPE = 32
MLA_V = 64
FFN_DIM = 2816
FFN_CHUNK = 256
FFN_NCHUNK = FFN_DIM // FFN_CHUNK
LANES = 128
HALO = 16
VMEM_LIMIT = 56 * 1024 * 1024


def _params(*sem):
    return pltpu.CompilerParams(dimension_semantics=sem, vmem_limit_bytes=VMEM_LIMIT)


def _const_spec(shape):
    nd = len(shape)
    return pl.BlockSpec(shape, lambda *_: (0,) * nd, pipeline_mode=pl.Buffered(1))


def _rms(x, g):
    return x * lax.rsqrt(jnp.mean(x * x, axis=-1, keepdims=True) + EPS) * g


def _dot(a, b):
    return jnp.dot(a, b, preferred_element_type=_F32)


def _dot_nt(a, b):
    return lax.dot_general(a, b, (((1,), (1,)), ((), ())), preferred_element_type=_F32)


def _mod_spec(mod_row):
    if mod_row is None:
        return pl.BlockSpec((1, N_MOD, D_MODEL), lambda b, i: (b, 0, 0))
    return pl.BlockSpec((1, N_MOD, D_MODEL), lambda b, i: (mod_row, 0, 0))


def _mod_kernel(c_ref, w_ref, b_ref, o_ref):
    c = c_ref[...]
    s = (c * jax.nn.sigmoid(c)).astype(_BF)
    o_ref[0] = _dot(s, w_ref[0].astype(_BF)) + b_ref[0]


def _modulation(cc, w_mod, b_mod):
    depth, d, n = w_mod.shape
    tn = 512
    return pl.pallas_call(
        _mod_kernel,
        grid=(depth, n // tn),
        in_specs=[pl.BlockSpec(cc.shape, lambda l, j: (0, 0)),
                  pl.BlockSpec((1, d, tn), lambda l, j: (l, 0, j)),
                  pl.BlockSpec((1, 1, tn), lambda l, j: (l, 0, j))],
        out_specs=pl.BlockSpec((1, cc.shape[0], tn), lambda l, j: (l, 0, j)),
        out_shape=jax.ShapeDtypeStruct((depth, cc.shape[0], n), _F32),
        compiler_params=_params("parallel", "parallel"),
        name="modulation",
    )(cc, w_mod, b_mod.reshape(depth, 1, n))


_AB_Q0, _AB_K0, _AB_V0, _AB_QP0, _AB_KP0, _AB_N = 1024, 1536, 1792, 1920, 2432, 2688


def _ab_in_kernel(x_ref, mod_ref, g_ref, w_ref, cos_ref, sin_ref, glu_ref, q_ref, k_ref, v_ref):
    h = (_rms(x_ref[0], g_ref[0:1]) * (1.0 + mod_ref[0, 1:2]) + mod_ref[0, 0:1]).astype(_BF)
    z = _dot(h, w_ref[...])
    glu_ref[0] = z[:, :_AB_Q0]
    cos = cos_ref[...]
    sin = sin_ref[...]
    for j in range(4):
        qf = z[:, _AB_Q0 + LANES * j:_AB_Q0 + LANES * (j + 1)]
        qp = z[:, _AB_QP0 + LANES * j:_AB_QP0 + LANES * (j + 1)]
        q_ref[0, j] = (qf * cos + qp * sin).astype(_BF)
        q_ref[0, 4 + j] = qf.astype(_BF)
    for j in range(2):
        kf = z[:, _AB_K0 + LANES * j:_AB_K0 + LANES * (j + 1)]
        kp = z[:, _AB_KP0 + LANES * j:_AB_KP0 + LANES * (j + 1)]
        k_ref[0, :, LANES * j:LANES * (j + 1)] = (kf * cos + kp * sin).astype(_BF)
    v_ref[0] = z[:, _AB_V0:_AB_QP0].astype(_BF)


def _ab_in(x, mod, mod_row, g, w_ext, cos, sin):
    b, t, d = x.shape
    tm = min(512, t)
    return pl.pallas_call(
        _ab_in_kernel,
        grid=(b, t // tm),
        in_specs=[pl.BlockSpec((1, tm, d), lambda b, i: (b, i, 0)),
                  _mod_spec(mod_row),
                  _const_spec(g.shape),
                  _const_spec(w_ext.shape),
                  pl.BlockSpec((tm, LANES), lambda b, i: (i, 0)),
                  pl.BlockSpec((tm, LANES), lambda b, i: (i, 0))],
        out_specs=[pl.BlockSpec((1, tm, 1024), lambda b, i: (b, i, 0)),
                   pl.BlockSpec((1, 8, tm, LANES), lambda b, i: (b, 0, i, 0)),
                   pl.BlockSpec((1, tm, 256), lambda b, i: (b, i, 0)),
                   pl.BlockSpec((1, tm, LANES), lambda b, i: (b, i, 0))],
        out_shape=[jax.ShapeDtypeStruct((b, t, 1024), _F32),
                   jax.ShapeDtypeStruct((b, 8, t, LANES), _BF),
                   jax.ShapeDtypeStruct((b, t, 256), _BF),
                   jax.ShapeDtypeStruct((b, t, LANES), _BF)],
        compiler_params=_params("parallel", "parallel"),
        name="ab_in",
    )(x, mod, g, w_ext, cos, sin)


_CONV_ROWS = 64


def _conv_kernel(glu_ref, prev_ref, next_ref, w_ref, cb_ref, lg_ref, lb_ref, o_ref, uext_ref, *, tt, nt):
    i = pl.program_id(1)

    def glu(z):
        return z[:, :CONV_A_DIM] * jax.nn.sigmoid(z[:, CONV_A_DIM:])

    uext_ref[0:HALO] = jnp.where(i > 0, glu(prev_ref[0]), 0.0)
    uext_ref[HALO:HALO + tt] = glu(glu_ref[0])
    uext_ref[HALO + tt:2 * HALO + tt] = jnp.where(i < nt - 1, glu(next_ref[0]), 0.0)
    half = CONV_A_WIDTH // 2
    for r in range(tt // _CONV_ROWS):
        acc = jnp.zeros((_CONV_ROWS, CONV_A_DIM), _F32)
        for j in range(CONV_A_WIDTH):
            off = r * _CONV_ROWS + HALO - half + j
            acc = acc + w_ref[j:j + 1, :] * uext_ref[off:off + _CONV_ROWS, :]
        u = acc + cb_ref[...]
        mu = jnp.mean(u, axis=-1, keepdims=True)
        var = jnp.mean(jnp.square(u - mu), axis=-1, keepdims=True)
        y = (u - mu) * lax.rsqrt(var + EPS) * lg_ref[...] + lb_ref[...]
        o_ref[0, r * _CONV_ROWS:(r + 1) * _CONV_ROWS, :] = (y * jax.nn.sigmoid(y)).astype(_BF)


def _conformer_conv(glu, conv_w, conv_b, ln_g, ln_b):
    b, t, _ = glu.shape
    tt = min(512, t)
    nt = t // tt
    hb = tt // HALO
    return pl.pallas_call(
        functools.partial(_conv_kernel, tt=tt, nt=nt),
        grid=(b, nt),
        in_specs=[pl.BlockSpec((1, tt, 1024), lambda b, i: (b, i, 0)),
                  pl.BlockSpec((1, HALO, 1024), lambda b, i: (b, jnp.maximum(i * hb - 1, 0), 0)),
                  pl.BlockSpec((1, HALO, 1024), lambda b, i: (b, jnp.minimum((i + 1) * hb, t // HALO - 1), 0)),
                  _const_spec(conv_w.shape),
                  _const_spec((1, CONV_A_DIM)), _const_spec((1, CONV_A_DIM)), _const_spec((1, CONV_A_DIM))],
        out_specs=pl.BlockSpec((1, tt, CONV_A_DIM), lambda b, i: (b, i, 0)),
        out_shape=jax.ShapeDtypeStruct((b, t, CONV_A_DIM), _BF),
        scratch_shapes=[pltpu.VMEM((tt + 2 * HALO, CONV_A_DIM), _F32)],
        compiler_params=_params("parallel", "parallel"),
        name="conformer_conv",
    )(glu, glu, glu, conv_w, conv_b.reshape(1, -1), ln_g.reshape(1, -1), ln_b.reshape(1, -1))


def _win_kernel(q_ref, kl_ref, vl_ref, kc_ref, vc_ref, se_ref, so_ref, o_ref, *, t, tq, has_lat):
    n = pl.program_id(1)
    rows = 4 * tq
    scale = WIN_HEAD_DIM ** -0.5
    q_main = q_ref[0, 0:4].reshape(rows, LANES)
    if has_lat:
        q_raw = q_ref[0, 4:8].reshape(rows, LANES)
        band = tq + 2 * WINDOW
        start = pl.multiple_of(jnp.clip(n * tq - WINDOW, 0, t - band), LANES)
        kb = kl_ref[0, pl.ds(start, band), :]
        vb = vl_ref[0, pl.ds(start, band), :]
        qpos = n * tq + (lax.broadcasted_iota(jnp.int32, (rows, 1), 0) & (tq - 1))
        kpos = start + lax.broadcasted_iota(jnp.int32, (1, band), 1)
        valid = jnp.abs(qpos - kpos) <= WINDOW
    else:
        q_raw = q_main
    kc = kc_ref[0]
    vc = vc_ref[0]
    outs = []
    for side, sink_ref in ((0, se_ref), (1, so_ref)):
        sink = sink_ref[...]
        lo, hi = side * LANES, (side + 1) * LANES
        s_ctx = _dot_nt(q_raw, kc[:, lo:hi]) * scale
        m = jnp.maximum(jnp.max(s_ctx, axis=-1, keepdims=True), sink)
        if has_lat:
            s_lat = jnp.where(valid, _dot_nt(q_main, kb[:, lo:hi]) * scale, NEG_INF)
            m = jnp.maximum(m, jnp.max(s_lat, axis=-1, keepdims=True))
        p_ctx = jnp.exp(s_ctx - m)
        den = jnp.sum(p_ctx, axis=-1, keepdims=True) + jnp.exp(sink - m)
        o = _dot(p_ctx.astype(_BF), vc)
        if has_lat:
            p_lat = jnp.exp(s_lat - m)
            den = den + jnp.sum(p_lat, axis=-1, keepdims=True)
            o = o + _dot(p_lat.astype(_BF), vb)
        outs.append(o / den)
    lane = lax.broadcasted_iota(jnp.int32, (rows, LANES), 1)
    o = jnp.where(lane < WIN_HEAD_DIM, outs[0], outs[1])
    for j in range(4):
        o_ref[0, :, j * LANES:(j + 1) * LANES] = o[j * tq:(j + 1) * tq].astype(_BF)


def _win_attention(q, k_lat, v_lat, k_ctx, v_ctx, sink, has_lat):
    b, _, t, _ = q.shape
    lc = k_ctx.shape[1]
    tq = 128 if has_lat else t
    tl = k_lat.shape[1]
    g = WIN_HEADS // WIN_KV_HEADS
    se = jnp.repeat(sink[:g], tq).reshape(4 * tq, 1)
    so = jnp.repeat(sink[g:], tq).reshape(4 * tq, 1)
    return pl.pallas_call(
        functools.partial(_win_kernel, t=t, tq=tq, has_lat=has_lat),
        grid=(b, t // tq),
        in_specs=[pl.BlockSpec((1, 8, tq, LANES), lambda b, n: (b, 0, n, 0)),
                  pl.BlockSpec((1, tl, 256), lambda b, n: (b, 0, 0)),
                  pl.BlockSpec((1, tl, LANES), lambda b, n: (b, 0, 0)),
                  pl.BlockSpec((1, lc, 256), lambda b, n: (b, 0, 0)),
                  pl.BlockSpec((1, lc, LANES), lambda b, n: (b, 0, 0)),
                  _const_spec((4 * tq, 1)), _const_spec((4 * tq, 1))],
        out_specs=pl.BlockSpec((1, tq, 512), lambda b, n: (b, n, 0)),
        out_shape=jax.ShapeDtypeStruct((b, t, 512), _BF),
        compiler_params=_params("parallel", "parallel"),
        name="win_attention" if has_lat else "ctx_attention",
    )(q, k_lat, v_lat, k_ctx, v_ctx, se, so)


def _out_kernel(a_ref, b_ref, wa_ref, wb_ref, x_ref, mod_ref, g_ref, xo_ref, h_ref):
    y = _dot(a_ref[0], wa_ref[...]) + _dot(b_ref[0], wb_ref[...])
    x1 = x_ref[0] + mod_ref[0, 2:3] * _rms(y, g_ref[1:2])
    xo_ref[0] = x1
    h_ref[0] = (_rms(x1, g_ref[2:3]) * (1.0 + mod_ref[0, 4:5]) + mod_ref[0, 3:4]).astype(_BF)


def _mixer_out(a, bb, wa, wb, x, mod, mod_row, g):
    b, t, d = x.shape
    tm = min(512, t)
    return pl.pallas_call(
        _out_kernel,
        grid=(b, t // tm),
        in_specs=[pl.BlockSpec((1, tm, 512), lambda b, i: (b, i, 0)),
                  pl.BlockSpec((1, tm, 512), lambda b, i: (b, i, 0)),
                  _const_spec(wa.shape), _const_spec(wb.shape),
                  pl.BlockSpec((1, tm, d), lambda b, i: (b, i, 0)),
                  _mod_spec(mod_row),
                  _const_spec(g.shape)],
        out_specs=[pl.BlockSpec((1, tm, d), lambda b, i: (b, i, 0)),
                   pl.BlockSpec((1, tm, d), lambda b, i: (b, i, 0))],
        out_shape=[jax.ShapeDtypeStruct((b, t, d), _F32),
                   jax.ShapeDtypeStruct((b, t, d), _BF)],
        compiler_params=_params("parallel", "parallel"),
        name="mixer_out",
    )(a, bb, wa, wb, x, mod, g)


def _ffn_kernel(h_ref, hp_ref, hn_ref, x_ref, wup_ref, cw_ref, cb_ref, wdn_ref, mod_ref, g_ref,
                o_ref, gbuf_ref, acc_ref, *, tm, nt):
    i = pl.program_id(1)
    cur = h_ref[0]
    lhs = jnp.concatenate([hp_ref[0], cur, hn_ref[0]], axis=0)
    row = lax.broadcasted_iota(jnp.int32, (tm + 2 * HALO, 1), 0)
    keep = jnp.logical_and(jnp.logical_or(i > 0, row >= HALO), jnp.logical_or(i < nt - 1, row < HALO + tm))
    acc_ref[...] = jnp.zeros_like(acc_ref)

    def chunk(c, carry):
        gbuf_ref[...] = jnp.where(keep, _dot(lhs, wup_ref[c]), 0.0)
        w = cw_ref[c]
        gc = (w[0:1] * gbuf_ref[HALO - 1:HALO - 1 + tm, :] + w[1:2] * gbuf_ref[HALO:HALO + tm, :]
              + w[2:3] * gbuf_ref[HALO + 1:HALO + 1 + tm, :] + cb_ref[c])
        u = _dot(cur, wup_ref[FFN_NCHUNK + c])
        act = (jax.nn.gelu(gc) * u).astype(_BF)
        acc_ref[...] += _dot(act, wdn_ref[c])
        return carry

    lax.fori_loop(0, FFN_NCHUNK, chunk, 0)
    o_ref[0] = x_ref[0] + mod_ref[0, 5:6] * _rms(acc_ref[...], g_ref[3:4])


def _ffn(h, x, wup, cw, cb, wdn, mod, mod_row, g):
    b, t, d = x.shape
    tm = min(512, t)
    nt = t // tm
    hb = tm // HALO
    return pl.pallas_call(
        functools.partial(_ffn_kernel, tm=tm, nt=nt),
        grid=(b, nt),
        in_specs=[pl.BlockSpec((1, tm, d), lambda b, i: (b, i, 0)),
                  pl.BlockSpec((1, HALO, d), lambda b, i: (b, jnp.maximum(i * hb - 1, 0), 0)),
                  pl.BlockSpec((1, HALO, d), lambda b, i: (b, jnp.minimum((i + 1) * hb, t // HALO - 1), 0)),
                  pl.BlockSpec((1, tm, d), lambda b, i: (b, i, 0)),
                  _const_spec(wup.shape), _const_spec(cw.shape), _const_spec(cb.shape), _const_spec(wdn.shape),
                  _mod_spec(mod_row),
                  _const_spec(g.shape)],
        out_specs=pl.BlockSpec((1, tm, d), lambda b, i: (b, i, 0)),
        out_shape=jax.ShapeDtypeStruct((b, t, d), _F32),
        scratch_shapes=[pltpu.VMEM((tm + 2 * HALO, FFN_CHUNK), _F32),
                        pltpu.VMEM((tm, d), _F32)],
        compiler_params=_params("parallel", "parallel"),
        name="conv_glu_ffn",
    )(h, h, h, x, wup, cw, cb, wdn, mod, g)


_CD_CQ0, _CD_CKV0, _CD_KR0, _CD_KRP0, _CD_N = 1024, 1408, 1664, 1792, 1920


def _cd_in_kernel(x_ref, mod_ref, g_ref, w1_ref, qn_ref, wq_ref, kvn_ref, wkv_ref,
                  cq_ref, sq_ref, ck_ref, sk_ref, xg_ref, q_ref, k_ref, v_ref):
    h = (_rms(x_ref[0], g_ref[0:1]) * (1.0 + mod_ref[0, 1:2]) + mod_ref[0, 0:1]).astype(_BF)
    z = _dot(h, w1_ref[...])
    xg_ref[0] = z[:, :_CD_CQ0]
    cqn = _rms(z[:, _CD_CQ0:_CD_CKV0], qn_ref[...]).astype(_BF)
    qq = _dot(cqn, wq_ref[...])
    cosq = cq_ref[...]
    sinq = sq_ref[...]
    nq = MLA_HEADS * LANES
    for hd in range(MLA_HEADS):
        lo, hi = hd * LANES, (hd + 1) * LANES
        q_ref[0, :, lo:hi] = (qq[:, lo:hi] * cosq + qq[:, nq + lo:nq + hi] * sinq).astype(_BF)
    ckvn = _rms(z[:, _CD_CKV0:_CD_KR0], kvn_ref[...]).astype(_BF)
    kv = _dot(ckvn, wkv_ref[...])
    krr = z[:, _CD_KR0:_CD_KRP0] * ck_ref[...] + z[:, _CD_KRP0:_CD_N] * sk_ref[...]
    for hd in range(MLA_HEADS):
        lo, hi = hd * LANES, (hd + 1) * LANES
        k_ref[0, :, lo:hi] = (kv[:, lo:hi] + krr).astype(_BF)
    v_ref[0] = kv[:, nq:].astype(_BF)


def _cd_in(x, mod, mod_row, g, w1, qn, wq, kvn, wkv, cosq, sinq, cosk, sink):
    b, t, d = x.shape
    tm = min(512, t)
    tab = pl.BlockSpec((tm, LANES), lambda b, i: (i, 0))
    return pl.pallas_call(
        _cd_in_kernel,
        grid=(b, t // tm),
        in_specs=[pl.BlockSpec((1, tm, d), lambda b, i: (b, i, 0)),
                  _mod_spec(mod_row),
                  _const_spec(g.shape), _const_spec(w1.shape), _const_spec(qn.shape), _const_spec(wq.shape),
                  _const_spec(kvn.shape), _const_spec(wkv.shape), tab, tab, tab, tab],
        out_specs=[pl.BlockSpec((1, tm, 1024), lambda b, i: (b, i, 0)),
                   pl.BlockSpec((1, tm, 1024), lambda b, i: (b, i, 0)),
                   pl.BlockSpec((1, tm, 1024), lambda b, i: (b, i, 0)),
                   pl.BlockSpec((1, tm, 512), lambda b, i: (b, i, 0))],
        out_shape=[jax.ShapeDtypeStruct((b, t, 1024), _F32),
                   jax.ShapeDtypeStruct((b, t, 1024), _BF),
                   jax.ShapeDtypeStruct((b, t, 1024), _BF),
                   jax.ShapeDtypeStruct((b, t, 512), _BF)],
        compiler_params=_params("parallel", "parallel"),
        name="cd_in",
    )(x, mod, g, w1, qn, wq, kvn, wkv, cosq, sinq, cosk, sink)


_SLAB = 8


def _lru_kernel(*refs, tt, reverse, combine):
    if combine:
        (x_ref, h0_ref, cw_ref, cb_ref, wg_ref, gb_ref, lam_ref, hb_ref, gt_ref,
         o_ref, xext_ref, a_ref, b_ref, hcar_ref, xcar_ref, hs_ref) = refs
    else:
        (x_ref, h0_ref, cw_ref, cb_ref, wg_ref, gb_ref, lam_ref,
         o_ref, xext_ref, a_ref, b_ref, hcar_ref, xcar_ref) = refs
        hs_ref = o_ref.at[0]
    i = pl.program_id(1)

    @pl.when(i == 0)
    def _():
        hcar_ref[...] = jnp.broadcast_to(h0_ref[0], (_SLAB, LRU_DIM))
        xcar_ref[...] = jnp.zeros_like(xcar_ref)

    x = x_ref[0]
    if reverse:
        xext_ref[0:tt] = x
        xext_ref[tt:tt + _SLAB] = xcar_ref[...]
        xcar_ref[...] = x[0:_SLAB]
        base = 0
    else:
        xext_ref[0:_SLAB] = xcar_ref[...]
        xext_ref[_SLAB:_SLAB + tt] = x
        xcar_ref[...] = x[tt - _SLAB:tt]
        base = _SLAB - (LRU_CONV_WIDTH - 1)
    xc = cb_ref[...]
    for j in range(LRU_CONV_WIDTH):
        xc = xc + cw_ref[j:j + 1, :] * xext_ref[base + j:base + j + tt, :]
    gates = _dot(xc.astype(_BF), wg_ref[...]) + gb_ref[...]
    r = jax.nn.sigmoid(gates[:, :LRU_DIM])
    ig = jax.nn.sigmoid(gates[:, LRU_DIM:])
    nl = -lam_ref[...]
    neg_sp = jnp.maximum(nl, 0.0) + jnp.log1p(jnp.exp(-jnp.abs(nl)))
    log_a = -LRU_C * r * neg_sp
    th = jnp.tanh(log_a)
    a_ref[...] = jnp.exp(log_a)
    b_ref[...] = jnp.sqrt(-2.0 * th / (1.0 - th)) * ig * xc

    ns = tt // _SLAB
    rowi = lax.broadcasted_iota(jnp.int32, (_SLAB, LRU_DIM), 0)

    def slab(s, h):
        r0 = pl.multiple_of(((ns - 1 - s) if reverse else s) * _SLAB, _SLAB)
        av = a_ref[pl.ds(r0, _SLAB), :]
        bv = b_ref[pl.ds(r0, _SLAB), :]
        for sh in (1, 2, 4):
            if reverse:
                ok = rowi < _SLAB - sh
                a_sh = pltpu.roll(av, _SLAB - sh, 0)
                b_sh = pltpu.roll(bv, _SLAB - sh, 0)
            else:
                ok = rowi >= sh
                a_sh = pltpu.roll(av, sh, 0)
                b_sh = pltpu.roll(bv, sh, 0)
            bv = jnp.where(ok, av * b_sh + bv, bv)
            av = jnp.where(ok, av * a_sh, av)
        hs = av * h + bv
        hs_ref[pl.ds(r0, _SLAB), :] = hs
        last = hs[0:1] if reverse else hs[_SLAB - 1:_SLAB]
        return jnp.broadcast_to(last, (_SLAB, LRU_DIM))

    hcar_ref[...] = lax.fori_loop(0, ns, slab, hcar_ref[...])
    if combine:
        o_ref[0] = ((hs_ref[...] + hb_ref[0]) * jax.nn.gelu(gt_ref[0])).astype(o_ref.dtype)


def _rglru(xg, h0, conv_w, conv_b, wg, gate_b, lam, reverse, hb=None):
    b, t, _ = xg.shape
    tt = min(256, t)
    nt = t // tt
    combine = hb is not None
    tidx = (lambda i: nt - 1 - i) if reverse else (lambda i: i)
    in_specs = [pl.BlockSpec((1, tt, LRU_DIM), lambda b, i: (b, tidx(i), 0)),
                pl.BlockSpec((1, 1, LRU_DIM), lambda b, i: (b, 0, 0)),
                _const_spec(conv_w.shape), _const_spec((1, LRU_DIM)), _const_spec(wg.shape),
                _const_spec((1, 2 * LRU_DIM)), _const_spec((1, LRU_DIM))]
    args = [xg, h0.reshape(b, 1, LRU_DIM), conv_w, conv_b.reshape(1, -1), wg, gate_b.reshape(1, -1), lam.reshape(1, -1)]
    if combine:
        in_specs += [pl.BlockSpec((1, tt, LRU_DIM), lambda b, i: (b, tidx(i), 0)),
                     pl.BlockSpec((1, tt, LRU_DIM), lambda b, i: (b, tidx(i), 1))]
        args += [hb, xg]
    return pl.pallas_call(
        functools.partial(_lru_kernel, tt=tt, reverse=reverse, combine=combine),
        grid=(b, nt),
        in_specs=in_specs,
        out_specs=pl.BlockSpec((1, tt, LRU_DIM), lambda b, i: (b, tidx(i), 0)),
        out_shape=jax.ShapeDtypeStruct((b, t, LRU_DIM), _BF if combine else _F32),
        scratch_shapes=[pltpu.VMEM((tt + _SLAB, LRU_DIM), _F32),
                        pltpu.VMEM((tt, LRU_DIM), _F32),
                        pltpu.VMEM((tt, LRU_DIM), _F32),
                        pltpu.VMEM((_SLAB, LRU_DIM), _F32),
                        pltpu.VMEM((_SLAB, LRU_DIM), _F32)]
        + ([pltpu.VMEM((tt, LRU_DIM), _F32)] if combine else []),
        compiler_params=_params("parallel", "arbitrary"),
        name="rglru_combine" if combine else ("rglru_rev" if reverse else "rglru_fwd"),
    )(*args)


_MLA_TK = 512


def _mla_kernel(q_ref, kl_ref, vl_ref, kc_ref, vc_ref, o_ref, *, t, tq):
    scale = (MLA_NOPE + MLA_ROPE) ** -0.5
    nchunk = t // _MLA_TK
    lane = lax.broadcasted_iota(jnp.int32, (tq, LANES), 1)

    def update(carry, q, k, v):
        m, l, acc = carry
        s = _dot_nt(q, k) * scale
        m_new = jnp.maximum(m, jnp.max(s, axis=-1, keepdims=True))
        alpha = jnp.exp(m - m_new)
        p = jnp.exp(s - m_new)
        l = alpha * l + jnp.sum(p, axis=-1, keepdims=True)
        acc = alpha * acc + _dot(p.astype(_BF), v)
        return m_new, l, acc

    for pair in range(MLA_HEADS // 2):
        vlo, vhi = pair * LANES, (pair + 1) * LANES
        outs = []
        for hd in (2 * pair, 2 * pair + 1):
            lo, hi = hd * LANES, (hd + 1) * LANES
            q = q_ref[0, :, lo:hi]

            def body(c, carry, q=q, lo=lo, hi=hi, vlo=vlo, vhi=vhi):
                r0 = pl.multiple_of(c * _MLA_TK, _MLA_TK)
                return update(carry, q, kl_ref[0, pl.ds(r0, _MLA_TK), lo:hi], vl_ref[0, pl.ds(r0, _MLA_TK), vlo:vhi])

            init = (jnp.full((tq, 1), NEG_INF, _F32), jnp.zeros((tq, 1), _F32), jnp.zeros((tq, LANES), _F32))
            carry = lax.fori_loop(0, nchunk, body, init)
            _, l, acc = update(carry, q, kc_ref[0, :, lo:hi], vc_ref[0, :, vlo:vhi])
            outs.append(acc / l)
        o_ref[0, :, vlo:vhi] = jnp.where(lane < MLA_V, outs[0], outs[1]).astype(_BF)


def _mla_attention(q, k_lat, v_lat, k_ctx, v_ctx):
    b, t, _ = q.shape
    lc = k_ctx.shape[1]
    tq = 256
    return pl.pallas_call(
        functools.partial(_mla_kernel, t=t, tq=tq),
        grid=(b, t // tq),
        in_specs=[pl.BlockSpec((1, tq, 1024), lambda b, i: (b, i, 0)),
                  pl.BlockSpec((1, t, 1024), lambda b, i: (b, 0, 0)),
                  pl.BlockSpec((1, t, 512), lambda b, i: (b, 0, 0)),
                  pl.BlockSpec((1, lc, 1024), lambda b, i: (b, 0, 0)),
                  pl.BlockSpec((1, lc, 512), lambda b, i: (b, 0, 0))],
        out_specs=pl.BlockSpec((1, tq, 512), lambda b, i: (b, i, 0)),
        out_shape=jax.ShapeDtypeStruct((b, t, 512), _BF),
        compiler_params=_params("parallel", "parallel"),
        name="mla_attention",
    )(q, k_lat, v_lat, k_ctx, v_ctx)


def _rope_partner_cols(w, dim):
    q = dim // 4
    return jnp.concatenate([-w[:, q:2 * q], w[:, :q], -w[:, 3 * q:], w[:, 2 * q:3 * q]], axis=1)


def _rope_tables(t, dim):
    rows = t // GRID_W
    row = jnp.repeat(jnp.arange(rows), GRID_W).astype(_F32)
    col = jnp.tile(jnp.arange(GRID_W), rows).astype(_F32)
    nf = dim // 4
    inv = ROPE_THETA ** (-jnp.arange(nf, dtype=_F32) / nf)
    ar = row[:, None] * inv
    ac = col[:, None] * inv
    ang = jnp.concatenate([ar, ar, ac, ac], axis=-1)
    return jnp.cos(ang), jnp.sin(ang)


def _ab_weights(w_in, w_out):
    hd = WIN_HEAD_DIM
    g = WIN_HEADS // WIN_KV_HEADS
    glu = w_in[:, :1024]
    wq = w_in[:, 1024:1536]
    wk = w_in[:, 1536:1664]
    wv = w_in[:, 1664:1792]
    zeros = jnp.zeros_like(wk[:, :hd])
    qh = [wq[:, h * hd:(h + 1) * hd] for h in range(WIN_HEADS)]
    kh = [wk[:, h * hd:(h + 1) * hd] for h in range(WIN_KV_HEADS)]
    part = lambda w: _rope_partner_cols(w, hd)
    q_full = jnp.concatenate([jnp.concatenate([qh[j], qh[g + j]], 1) for j in range(g)], 1)
    q_part = jnp.concatenate([jnp.concatenate([part(qh[j]), part(qh[g + j])], 1) for j in range(g)], 1)
    k_full = jnp.concatenate([kh[0], zeros, zeros, kh[1]], 1)
    k_part = jnp.concatenate([part(kh[0]), zeros, zeros, part(kh[1])], 1)
    w_ext = jnp.concatenate([glu, q_full, k_full, wv, q_part, k_part], 1).astype(_BF)
    wa = w_out[:CONV_A_DIM].astype(_BF)
    wb_rows = w_out[CONV_A_DIM:]
    wb = jnp.concatenate([jnp.concatenate([wb_rows[j * hd:(j + 1) * hd], wb_rows[(g + j) * hd:(g + j + 1) * hd]], 0)
                          for j in range(g)], 0).astype(_BF)
    return w_ext, wa, wb


def _cd_weights(w_in, w_uq, w_ukv, lat):
    d = w_in.shape[0]
    kr = w_in[:, 1664:1696]
    z32 = jnp.zeros((d, MLA_ROPE), w_in.dtype)
    z64 = jnp.zeros((d, MLA_NOPE), w_in.dtype)
    if lat:
        kr128 = jnp.concatenate([z64, kr, z32], 1)
        krp128 = jnp.concatenate([z64, _rope_partner_cols(kr, MLA_ROPE), z32], 1)
    else:
        kr128 = jnp.concatenate([z64, z32, kr], 1)
        krp128 = jnp.zeros((d, LANES), w_in.dtype)
    w1 = jnp.concatenate([w_in[:, :1664], kr128, krp128], 1).astype(_BF)
    qd = MLA_NOPE + MLA_ROPE
    zq64 = jnp.zeros((MLA_Q_RANK, MLA_NOPE), w_uq.dtype)
    zq32 = jnp.zeros((MLA_Q_RANK, MLA_ROPE), w_uq.dtype)
    full, part = [], []
    for h in range(MLA_HEADS):
        nope = w_uq[:, h * qd:h * qd + MLA_NOPE]
        rope = w_uq[:, h * qd + MLA_NOPE:(h + 1) * qd]
        full += [nope, rope, rope]
        part += [zq64, _rope_partner_cols(rope, MLA_ROPE), zq32]
    wq = jnp.concatenate(full + part, 1).astype(_BF)
    kd = MLA_NOPE + MLA_V
    zk = jnp.zeros((MLA_KV_RANK, LANES - MLA_NOPE), w_ukv.dtype)
    kcols, vcols = [], []
    for h in range(MLA_HEADS):
        kcols += [w_ukv[:, h * kd:h * kd + MLA_NOPE], zk]
        vcols += [w_ukv[:, h * kd + MLA_NOPE:(h + 1) * kd]]
    wkv = jnp.concatenate(kcols + vcols, 1).astype(_BF)
    return w1, wq, wkv


def _ffn_weights(w_up, conv_w, conv_b, w_down):
    d = w_up.shape[0]
    wup = w_up.reshape(d, 2 * FFN_NCHUNK, FFN_CHUNK).transpose(1, 0, 2).astype(_BF)
    cw = conv_w.reshape(3, FFN_NCHUNK, FFN_CHUNK).transpose(1, 0, 2)
    cb = conv_b.reshape(FFN_NCHUNK, 1, FFN_CHUNK)
    wdn = w_down.reshape(FFN_NCHUNK, FFN_CHUNK, d).astype(_BF)
    return wup, cw, cb, wdn


def _block_diag_dense(w):
    nb, hh, kk = w.shape
    eye = jnp.eye(nb, dtype=w.dtype)
    return (eye[:, None, :, None] * w[:, :, None, :]).reshape(nb * hh, nb * kk)


def kernel(x, c, ctx, c_ctx, w_mod, b_mod, norm_g, ffn_w_up, ffn_conv_w, ffn_conv_b, ffn_w_down, ab_w_in, a_conv_w, a_conv_b, a_ln_g, a_ln_b, b_sink, ab_w_out, cd_w_in, lru_conv_w, lru_conv_b, lru_gate_w, lru_gate_b, lru_lambda, mla_q_norm, mla_w_uq, mla_kv_norm, mla_w_ukv, cd_w_out):
    b, t, d = x.shape
    lc = ctx.shape[1]
    ctx_row = 8 * ((b + 7) // 8)
    cc = jnp.zeros((ctx_row + 8, d), _F32).at[:b].set(c).at[ctx_row].set(c_ctx)
    mods = _modulation(cc, w_mod, b_mod).reshape(w_mod.shape[0], ctx_row + 8, N_MOD, d)

    mod, g = mods[0], norm_g[0]
    w_ext, wa, wb = _ab_weights(ab_w_in[0], ab_w_out[0])
    cos64, sin64 = _rope_tables(t, WIN_HEAD_DIM)
    cos_l = jnp.concatenate([cos64, cos64], 1)
    sin_l = jnp.concatenate([sin64, sin64], 1)
    cos_c = jnp.ones((lc, LANES), _F32)
    sin_c = jnp.zeros((lc, LANES), _F32)
    glu_l, q_l, k_l, v_l = _ab_in(x, mod, None, g, w_ext, cos_l, sin_l)
    glu_c, q_c, k_c, v_c = _ab_in(ctx, mod, ctx_row, g, w_ext, cos_c, sin_c)
    a_l = _conformer_conv(glu_l, a_conv_w[0], a_conv_b[0], a_ln_g[0], a_ln_b[0])
    a_c = _conformer_conv(glu_c, a_conv_w[0], a_conv_b[0], a_ln_g[0], a_ln_b[0])
    b_l = _win_attention(q_l, k_l, v_l, k_c, v_c, b_sink[0], True)
    b_c = _win_attention(q_c, k_c, v_c, k_c, v_c, b_sink[0], False)
    xl, hl = _mixer_out(a_l, b_l, wa, wb, x, mod, None, g)
    xc, hc = _mixer_out(a_c, b_c, wa, wb, ctx, mod, ctx_row, g)
    fw = _ffn_weights(ffn_w_up[0], ffn_conv_w[0], ffn_conv_b[0], ffn_w_down[0])
    xl = _ffn(hl, xl, *fw, mod, None, g)
    xc = _ffn(hc, xc, *fw, mod, ctx_row, g)

    mod, g = mods[1], norm_g[1]
    w1_l, wq, wkv = _cd_weights(cd_w_in[0], mla_w_uq[0], mla_w_ukv[0], True)
    w1_c, _, _ = _cd_weights(cd_w_in[0], mla_w_uq[0], mla_w_ukv[0], False)
    cos32, sin32 = _rope_tables(t, MLA_ROPE)
    one64, zero64 = jnp.ones((t, MLA_NOPE), _F32), jnp.zeros((t, MLA_NOPE), _F32)
    one32, zero32 = jnp.ones((t, MLA_ROPE), _F32), jnp.zeros((t, MLA_ROPE), _F32)
    cos_t = jnp.concatenate([one64, cos32, one32], 1)
    sin_t = jnp.concatenate([zero64, sin32, zero32], 1)
    qn = mla_q_norm[0].reshape(1, -1)
    kvn = mla_kv_norm[0].reshape(1, -1)
    xg_l, q_l, k_l, v_l = _cd_in(xl, mod, None, g, w1_l, qn, wq, kvn, wkv, cos_t, sin_t, cos_t, sin_t)
    xg_c, _, k_c, v_c = _cd_in(xc, mod, ctx_row, g, w1_c, qn, wq, kvn, wkv, cos_c, sin_c, cos_c, sin_c)
    wg = [jnp.concatenate([_block_diag_dense(lru_gate_w[0, dr, 0]), _block_diag_dense(lru_gate_w[0, dr, 1])], 1).astype(_BF)
          for dr in range(2)]
    gb = [jnp.concatenate([lru_gate_b[0, dr, 0], lru_gate_b[0, dr, 1]]) for dr in range(2)]
    lru = lambda xg, h0, dr, hb=None: _rglru(xg, h0, lru_conv_w[0, dr], lru_conv_b[0, dr], wg[dr], gb[dr],
                                             lru_lambda[0, dr], dr == 1, hb)
    h_zero = jnp.zeros((b, LRU_DIM), _F32)
    hf_c = lru(xg_c, h_zero, 0)
    hb_c = lru(xg_c, h_zero, 1)
    hb_l = lru(xg_l, hb_c[:, 0], 1)
    c_l = lru(xg_l, hf_c[:, lc - 1], 0, hb_l)
    d_l = _mla_attention(q_l, k_l, v_l, k_c, v_c)
    wo = cd_w_out[0]
    xl, hl = _mixer_out(c_l, d_l, wo[:LRU_DIM].astype(_BF), wo[LRU_DIM:].astype(_BF), xl, mod, None, g)
    fw = _ffn_weights(ffn_w_up[1], ffn_conv_w[1], ffn_conv_b[1], ffn_w_down[1])
    return _ffn(hl, xl, *fw, mod, None, g)
```

```python
import functools

import jax
import jax.numpy as jnp
from jax import lax
from jax.experimental import pallas as pl
from jax.experimental.pallas import tpu as pltpu

_BF = jnp.bfloat16
_F32 = jnp.float32

D_MODEL = 1024
N_MOD = 6
EPS = 1e-6
ROPE_THETA = 10000.0
GRID_W = 64
NEG_INF = -1e30

CONV_A_DIM = 512
CONV_A_WIDTH = 31
WIN_HEADS = 8
WIN_KV_HEADS = 2
WIN_HEAD_DIM = 64
WINDOW = 128
LRU_DIM = 512
LRU_BLOCKS = 8
LRU_CONV_WIDTH = 4
LRU_C = 8.0
MLA_HEADS = 8
MLA_Q_RANK = 384
MLA_KV_RANK = 256
MLA_NOPE = 64
MLA_ROPE = 32
MLA_V = 64
FFN_DIM = 2816
FFN_CHUNK = 256
FFN_NCHUNK = FFN_DIM // FFN_CHUNK
LANES = 128
HALO = 16
VMEM_LIMIT = 56 * 1024 * 1024
_LOG2E = 1.4426950408889634
_MLA_TK = 512
_MLA_VT = 80


def _params(*sem):
    return pltpu.CompilerParams(dimension_semantics=sem, vmem_limit_bytes=VMEM_LIMIT)


def _const_spec(shape):
    nd = len(shape)
    return pl.BlockSpec(shape, lambda *_: (0,) * nd, pipeline_mode=pl.Buffered(1))


def _rms(x, g):
    return x * lax.rsqrt(jnp.mean(x * x, axis=-1, keepdims=True) + EPS) * g


def _dot(a, b):
    return jnp.dot(a, b, preferred_element_type=_F32)


def _dot_nt(a, b):
    return lax.dot_general(a, b, (((1,), (1,)), ((), ())), preferred_element_type=_F32)


def _mod_spec(mod_row):
    if mod_row is None:
        return pl.BlockSpec((1, N_MOD, D_MODEL), lambda b, i: (b, 0, 0))
    return pl.BlockSpec((1, N_MOD, D_MODEL), lambda b, i: (mod_row, 0, 0))


def _mod_kernel(c_ref, w_ref, b_ref, o_ref):
    c = c_ref[...]
    s = (c * jax.nn.sigmoid(c)).astype(_BF)
    o_ref[0] = _dot(s, w_ref[0].astype(_BF)) + b_ref[0]


def _modulation(cc, w_mod, b_mod):
    depth, d, n = w_mod.shape
    tn = 512
    return pl.pallas_call(
        _mod_kernel,
        grid=(depth, n // tn),
        in_specs=[pl.BlockSpec(cc.shape, lambda l, j: (0, 0)),
                  pl.BlockSpec((1, d, tn), lambda l, j: (l, 0, j)),
                  pl.BlockSpec((1, 1, tn), lambda l, j: (l, 0, j))],
        out_specs=pl.BlockSpec((1, cc.shape[0], tn), lambda l, j: (l, 0, j)),
        out_shape=jax.ShapeDtypeStruct((depth, cc.shape[0], n), _F32),
        compiler_params=_params("parallel", "parallel"),
        name="modulation",
    )(cc, w_mod, b_mod.reshape(depth, 1, n))


_AB_Q0, _AB_K0, _AB_V0, _AB_QP0, _AB_KP0, _AB_N = 1024, 1536, 1792, 1920, 2432, 2688


def _ab_in_kernel(x_ref, mod_ref, g_ref, w_ref, cos_ref, sin_ref, glu_ref, q_ref, k_ref, v_ref):
    h = (_rms(x_ref[0], g_ref[0:1]) * (1.0 + mod_ref[0, 1:2]) + mod_ref[0, 0:1]).astype(_BF)
    z = _dot(h, w_ref[...])
    glu_ref[0] = z[:, :_AB_Q0]
    cos = cos_ref[...]
    sin = sin_ref[...]
    for j in range(4):
        qf = z[:, _AB_Q0 + LANES * j:_AB_Q0 + LANES * (j + 1)]
        qp = z[:, _AB_QP0 + LANES * j:_AB_QP0 + LANES * (j + 1)]
        q_ref[0, j] = (qf * cos + qp * sin).astype(_BF)
        q_ref[0, 4 + j] = qf.astype(_BF)
    for j in range(2):
        kf = z[:, _AB_K0 + LANES * j:_AB_K0 + LANES * (j + 1)]
        kp = z[:, _AB_KP0 + LANES * j:_AB_KP0 + LANES * (j + 1)]
        k_ref[0, :, LANES * j:LANES * (j + 1)] = (kf * cos + kp * sin).astype(_BF)
    v_ref[0] = z[:, _AB_V0:_AB_QP0].astype(_BF)


def _ab_in(x, mod, mod_row, g, w_ext, cos, sin):
    b, t, d = x.shape
    tm = min(512, t)
    return pl.pallas_call(
        _ab_in_kernel,
        grid=(b, t // tm),
        in_specs=[pl.BlockSpec((1, tm, d), lambda b, i: (b, i, 0)),
                  _mod_spec(mod_row),
                  _const_spec(g.shape),
                  _const_spec(w_ext.shape),
                  pl.BlockSpec((tm, LANES), lambda b, i: (i, 0)),
                  pl.BlockSpec((tm, LANES), lambda b, i: (i, 0))],
        out_specs=[pl.BlockSpec((1, tm, 1024), lambda b, i: (b, i, 0)),
                   pl.BlockSpec((1, 8, tm, LANES), lambda b, i: (b, 0, i, 0)),
                   pl.BlockSpec((1, tm, 256), lambda b, i: (b, i, 0)),
                   pl.BlockSpec((1, tm, LANES), lambda b, i: (b, i, 0))],
        out_shape=[jax.ShapeDtypeStruct((b, t, 1024), _F32),
                   jax.ShapeDtypeStruct((b, 8, t, LANES), _BF),
                   jax.ShapeDtypeStruct((b, t, 256), _BF),
                   jax.ShapeDtypeStruct((b, t, LANES), _BF)],
        compiler_params=_params("parallel", "parallel"),
        name="ab_in",
    )(x, mod, g, w_ext, cos, sin)


_CONV_ROWS = 64


def _conv_kernel(glu_ref, prev_ref, next_ref, w_ref, cb_ref, lg_ref, lb_ref, o_ref, uext_ref, *, tt, nt):
    i = pl.program_id(1)

    def glu(z):
        return z[:, :CONV_A_DIM] * jax.nn.sigmoid(z[:, CONV_A_DIM:])

    uext_ref[0:HALO] = jnp.where(i > 0, glu(prev_ref[0]), 0.0)
    uext_ref[HALO:HALO + tt] = glu(glu_ref[0])
    uext_ref[HALO + tt:2 * HALO + tt] = jnp.where(i < nt - 1, glu(next_ref[0]), 0.0)
    half = CONV_A_WIDTH // 2
    for r in range(tt // _CONV_ROWS):
        acc = jnp.zeros((_CONV_ROWS, CONV_A_DIM), _F32)
        for j in range(CONV_A_WIDTH):
            off = r * _CONV_ROWS + HALO - half + j
            acc = acc + w_ref[j:j + 1, :] * uext_ref[off:off + _CONV_ROWS, :]
        u = acc + cb_ref[...]
        mu = jnp.mean(u, axis=-1, keepdims=True)
        var = jnp.mean(jnp.square(u - mu), axis=-1, keepdims=True)
        y = (u - mu) * lax.rsqrt(var + EPS) * lg_ref[...] + lb_ref[...]
        o_ref[0, r * _CONV_ROWS:(r + 1) * _CONV_ROWS, :] = (y * jax.nn.sigmoid(y)).astype(_BF)


def _conformer_conv(glu, conv_w, conv_b, ln_g, ln_b):
    b, t, _ = glu.shape
    tt = min(512, t)
    nt = t // tt
    hb = tt // HALO
    return pl.pallas_call(
        functools.partial(_conv_kernel, tt=tt, nt=nt),
        grid=(b, nt),
        in_specs=[pl.BlockSpec((1, tt, 1024), lambda b, i: (b, i, 0)),
                  pl.BlockSpec((1, HALO, 1024), lambda b, i: (b, jnp.maximum(i * hb - 1, 0), 0)),
                  pl.BlockSpec((1, HALO, 1024), lambda b, i: (b, jnp.minimum((i + 1) * hb, t // HALO - 1), 0)),
                  _const_spec(conv_w.shape),
                  _const_spec((1, CONV_A_DIM)), _const_spec((1, CONV_A_DIM)), _const_spec((1, CONV_A_DIM))],
        out_specs=pl.BlockSpec((1, tt, CONV_A_DIM), lambda b, i: (b, i, 0)),
        out_shape=jax.ShapeDtypeStruct((b, t, CONV_A_DIM), _BF),
        scratch_shapes=[pltpu.VMEM((tt + 2 * HALO, CONV_A_DIM), _F32)],
        compiler_params=_params("parallel", "parallel"),
        name="conformer_conv",
    )(glu, glu, glu, conv_w, conv_b.reshape(1, -1), ln_g.reshape(1, -1), ln_b.reshape(1, -1))


def _win_kernel(q_ref, kl_ref, vl_ref, kc_ref, vc_ref, se_ref, so_ref, o_ref, *, t, tq, has_lat):
    n = pl.program_id(1)
    rows = 4 * tq
    scale = WIN_HEAD_DIM ** -0.5
    q_main = q_ref[0, 0:4].reshape(rows, LANES)
    if has_lat:
        q_raw = q_ref[0, 4:8].reshape(rows, LANES)
        band = tq + 2 * WINDOW
        start = pl.multiple_of(jnp.clip(n * tq - WINDOW, 0, t - band), LANES)
        kb = kl_ref[0, pl.ds(start, band), :]
        vb = vl_ref[0, pl.ds(start, band), :]
        qpos = n * tq + (lax.broadcasted_iota(jnp.int32, (rows, 1), 0) & (tq - 1))
        kpos = start + lax.broadcasted_iota(jnp.int32, (1, band), 1)
        valid = jnp.abs(qpos - kpos) <= WINDOW
    else:
        q_raw = q_main
    kc = kc_ref[0]
    vc = vc_ref[0]
    outs = []
    for side, sink_ref in ((0, se_ref), (1, so_ref)):
        sink = sink_ref[...]
        lo, hi = side * LANES, (side + 1) * LANES
        s_ctx = _dot_nt(q_raw, kc[:, lo:hi]) * scale
        m = jnp.maximum(jnp.max(s_ctx, axis=-1, keepdims=True), sink)
        if has_lat:
            s_lat = jnp.where(valid, _dot_nt(q_main, kb[:, lo:hi]) * scale, NEG_INF)
            m = jnp.maximum(m, jnp.max(s_lat, axis=-1, keepdims=True))
        p_ctx = jnp.exp(s_ctx - m)
        den = jnp.sum(p_ctx, axis=-1, keepdims=True) + jnp.exp(sink - m)
        o = _dot(p_ctx.astype(_BF), vc)
        if has_lat:
            p_lat = jnp.exp(s_lat - m)
            den = den + jnp.sum(p_lat, axis=-1, keepdims=True)
            o = o + _dot(p_lat.astype(_BF), vb)
        outs.append(o / den)
    lane = lax.broadcasted_iota(jnp.int32, (rows, LANES), 1)
    o = jnp.where(lane < WIN_HEAD_DIM, outs[0], outs[1])
    for j in range(4):
        o_ref[0, :, j * LANES:(j + 1) * LANES] = o[j * tq:(j + 1) * tq].astype(_BF)


def _win_attention(q, k_lat, v_lat, k_ctx, v_ctx, sink, has_lat):
    b, _, t, _ = q.shape
    lc = k_ctx.shape[1]
    tq = 128 if has_lat else t
    tl = k_lat.shape[1]
    g = WIN_HEADS // WIN_KV_HEADS
    se = jnp.repeat(sink[:g], tq).reshape(4 * tq, 1)
    so = jnp.repeat(sink[g:], tq).reshape(4 * tq, 1)
    return pl.pallas_call(
        functools.partial(_win_kernel, t=t, tq=tq, has_lat=has_lat),
        grid=(b, t // tq),
        in_specs=[pl.BlockSpec((1, 8, tq, LANES), lambda b, n: (b, 0, n, 0)),
                  pl.BlockSpec((1, tl, 256), lambda b, n: (b, 0, 0)),
                  pl.BlockSpec((1, tl, LANES), lambda b, n: (b, 0, 0)),
                  pl.BlockSpec((1, lc, 256), lambda b, n: (b, 0, 0)),
                  pl.BlockSpec((1, lc, LANES), lambda b, n: (b, 0, 0)),
                  _const_spec((4 * tq, 1)), _const_spec((4 * tq, 1))],
        out_specs=pl.BlockSpec((1, tq, 512), lambda b, n: (b, n, 0)),
        out_shape=jax.ShapeDtypeStruct((b, t, 512), _BF),
        compiler_params=_params("parallel", "parallel"),
        name="win_attention" if has_lat else "ctx_attention",
    )(q, k_lat, v_lat, k_ctx, v_ctx, se, so)


def _out_kernel(a_ref, b_ref, wa_ref, wb_ref, x_ref, mod_ref, g_ref, xo_ref, h_ref):
    y = _dot(a_ref[0], wa_ref[...]) + _dot(b_ref[0], wb_ref[...])
    x1 = x_ref[0] + mod_ref[0, 2:3] * _rms(y, g_ref[1:2])
    xo_ref[0] = x1
    h_ref[0] = (_rms(x1, g_ref[2:3]) * (1.0 + mod_ref[0, 4:5]) + mod_ref[0, 3:4]).astype(_BF)


def _mixer_out(a, bb, wa, wb, x, mod, mod_row, g):
    b, t, d = x.shape
    tm = min(512, t)
    return pl.pallas_call(
        _out_kernel,
        grid=(b, t // tm),
        in_specs=[pl.BlockSpec((1, tm, 512), lambda b, i: (b, i, 0)),
                  pl.BlockSpec((1, tm, 512), lambda b, i: (b, i, 0)),
                  _const_spec(wa.shape), _const_spec(wb.shape),
                  pl.BlockSpec((1, tm, d), lambda b, i: (b, i, 0)),
                  _mod_spec(mod_row),
                  _const_spec(g.shape)],
        out_specs=[pl.BlockSpec((1, tm, d), lambda b, i: (b, i, 0)),
                   pl.BlockSpec((1, tm, d), lambda b, i: (b, i, 0))],
        out_shape=[jax.ShapeDtypeStruct((b, t, d), _F32),
                   jax.ShapeDtypeStruct((b, t, d), _BF)],
        compiler_params=_params("parallel", "parallel"),
        name="mixer_out",
    )(a, bb, wa, wb, x, mod, g)


def _ffn_kernel(h_ref, hp_ref, hn_ref, x_ref, wup_ref, cw_ref, cb_ref, wdn_ref, mod_ref, g_ref,
                o_ref, gbuf_ref, acc_ref, *, tm, nt):
    i = pl.program_id(1)
    cur = h_ref[0]
    lhs = jnp.concatenate([hp_ref[0], cur, hn_ref[0]], axis=0)
    row = lax.broadcasted_iota(jnp.int32, (tm + 2 * HALO, 1), 0)
    keep = jnp.logical_and(jnp.logical_or(i > 0, row >= HALO), jnp.logical_or(i < nt - 1, row < HALO + tm))
    acc_ref[...] = jnp.zeros_like(acc_ref)

    def chunk(c, carry):
        gbuf_ref[...] = jnp.where(keep, _dot(lhs, wup_ref[c]), 0.0)
        w = cw_ref[c]
        gc = (w[0:1] * gbuf_ref[HALO - 1:HALO - 1 + tm, :] + w[1:2] * gbuf_ref[HALO:HALO + tm, :]
              + w[2:3] * gbuf_ref[HALO + 1:HALO + 1 + tm, :] + cb_ref[c])
        u = _dot(cur, wup_ref[FFN_NCHUNK + c])
        act = (jax.nn.gelu(gc) * u).astype(_BF)
        acc_ref[...] += _dot(act, wdn_ref[c])
        return carry

    lax.fori_loop(0, FFN_NCHUNK, chunk, 0)
    o_ref[0] = x_ref[0] + mod_ref[0, 5:6] * _rms(acc_ref[...], g_ref[3:4])


def _ffn(h, x, wup, cw, cb, wdn, mod, mod_row, g):
    b, t, d = x.shape
    tm = min(512, t)
    nt = t // tm
    hb = tm // HALO
    return pl.pallas_call(
        functools.partial(_ffn_kernel, tm=tm, nt=nt),
        grid=(b, nt),
        in_specs=[pl.BlockSpec((1, tm, d), lambda b, i: (b, i, 0)),
                  pl.BlockSpec((1, HALO, d), lambda b, i: (b, jnp.maximum(i * hb - 1, 0), 0)),
                  pl.BlockSpec((1, HALO, d), lambda b, i: (b, jnp.minimum((i + 1) * hb, t // HALO - 1), 0)),
                  pl.BlockSpec((1, tm, d), lambda b, i: (b, i, 0)),
                  _const_spec(wup.shape), _const_spec(cw.shape), _const_spec(cb.shape), _const_spec(wdn.shape),
                  _mod_spec(mod_row),
                  _const_spec(g.shape)],
        out_specs=pl.BlockSpec((1, tm, d), lambda b, i: (b, i, 0)),
        out_shape=jax.ShapeDtypeStruct((b, t, d), _F32),
        scratch_shapes=[pltpu.VMEM((tm + 2 * HALO, FFN_CHUNK), _F32),
                        pltpu.VMEM((tm, d), _F32)],
        compiler_params=_params("parallel", "parallel"),
        name="conv_glu_ffn",
    )(h, h, h, x, wup, cw, cb, wdn, mod, g)


_CD_CQ0, _CD_CKV0, _CD_KR0, _CD_KRP0, _CD_N = 1024, 1408, 1664, 1792, 1920


def _cd_in_kernel(x_ref, mod_ref, g_ref, w1_ref, qn_ref, wq_ref, kvn_ref, wkv_ref,
                  cq_ref, sq_ref, ck_ref, sk_ref, xg_ref, q_ref, k_ref, v_ref):
    h = (_rms(x_ref[0], g_ref[0:1]) * (1.0 + mod_ref[0, 1:2]) + mod_ref[0, 0:1]).astype(_BF)
    z = _dot(h, w1_ref[...])
    xg_ref[0] = z[:, :_CD_CQ0]
    cqn = _rms(z[:, _CD_CQ0:_CD_CKV0], qn_ref[...]).astype(_BF)
    qq = _dot(cqn, wq_ref[...])
    cosq = cq_ref[...]
    sinq = sq_ref[...]
    nq = MLA_HEADS * LANES
    for hd in range(MLA_HEADS):
        lo, hi = hd * LANES, (hd + 1) * LANES
        q_ref[0, :, lo:hi] = (qq[:, lo:hi] * cosq + qq[:, nq + lo:nq + hi] * sinq).astype(_BF)
    ckvn = _rms(z[:, _CD_CKV0:_CD_KR0], kvn_ref[...]).astype(_BF)
    kv = _dot(ckvn, wkv_ref[...])
    krr = z[:, _CD_KR0:_CD_KRP0] * ck_ref[...] + z[:, _CD_KRP0:_CD_N] * sk_ref[...]
    for hd in range(MLA_HEADS):
        lo, hi = hd * LANES, (hd + 1) * LANES
        k_ref[0, :, lo:hi] = (kv[:, lo:hi] + krr).astype(_BF)
    vt = jnp.transpose(kv[:, nq:]).astype(_BF)
    ones = jnp.ones((_MLA_VT - MLA_V, vt.shape[1]), _BF)
    for hd in range(MLA_HEADS):
        v_ref[0, 0, hd * _MLA_VT:hd * _MLA_VT + MLA_V, :] = vt[hd * MLA_V:(hd + 1) * MLA_V]
        v_ref[0, 0, hd * _MLA_VT + MLA_V:(hd + 1) * _MLA_VT, :] = ones


def _cd_in(x, mod, mod_row, g, w1, qn, wq, kvn, wkv, cosq, sinq, cosk, sink):
    b, t, d = x.shape
    tm = min(512, t)
    tab = pl.BlockSpec((tm, LANES), lambda b, i: (i, 0))
    return pl.pallas_call(
        _cd_in_kernel,
        grid=(b, t // tm),
        in_specs=[pl.BlockSpec((1, tm, d), lambda b, i: (b, i, 0)),
                  _mod_spec(mod_row),
                  _const_spec(g.shape), _const_spec(w1.shape), _const_spec(qn.shape), _const_spec(wq.shape),
                  _const_spec(kvn.shape), _const_spec(wkv.shape), tab, tab, tab, tab],
        out_specs=[pl.BlockSpec((1, tm, 1024), lambda b, i: (b, i, 0)),
                   pl.BlockSpec((1, tm, 1024), lambda b, i: (b, i, 0)),
                   pl.BlockSpec((1, tm, 1024), lambda b, i: (b, i, 0)),
                   pl.BlockSpec((1, 1, MLA_HEADS * _MLA_VT, tm), lambda b, i: (b, i, 0, 0))],
        out_shape=[jax.ShapeDtypeStruct((b, t, 1024), _F32),
                   jax.ShapeDtypeStruct((b, t, 1024), _BF),
                   jax.ShapeDtypeStruct((b, t, 1024), _BF),
                   jax.ShapeDtypeStruct((b, t // tm, MLA_HEADS * _MLA_VT, tm), _BF)],
        compiler_params=_params("parallel", "parallel"),
        name="cd_in",
    )(x, mod, g, w1, qn, wq, kvn, wkv, cosq, sinq, cosk, sink)


_SLAB = 8


def _lru_kernel(*refs, tt, reverse, combine):
    if combine:
        (x_ref, h0_ref, cw_ref, cb_ref, wg_ref, gb_ref, lam_ref, hb_ref, gt_ref,
         o_ref, xext_ref, a_ref, b_ref, hcar_ref, xcar_ref, hs_ref) = refs
    else:
        (x_ref, h0_ref, cw_ref, cb_ref, wg_ref, gb_ref, lam_ref,
         o_ref, xext_ref, a_ref, b_ref, hcar_ref, xcar_ref) = refs
        hs_ref = o_ref.at[0]
    i = pl.program_id(1)

    @pl.when(i == 0)
    def _():
        hcar_ref[...] = jnp.broadcast_to(h0_ref[0], (_SLAB, LRU_DIM))
        xcar_ref[...] = jnp.zeros_like(xcar_ref)

    x = x_ref[0]
    if reverse:
        xext_ref[0:tt] = x
        xext_ref[tt:tt + _SLAB] = xcar_ref[...]
        xcar_ref[...] = x[0:_SLAB]
        base = 0
    else:
        xext_ref[0:_SLAB] = xcar_ref[...]
        xext_ref[_SLAB:_SLAB + tt] = x
        xcar_ref[...] = x[tt - _SLAB:tt]
        base = _SLAB - (LRU_CONV_WIDTH - 1)
    xc = cb_ref[...]
    for j in range(LRU_CONV_WIDTH):
        xc = xc + cw_ref[j:j + 1, :] * xext_ref[base + j:base + j + tt, :]
    gates = _dot(xc.astype(_BF), wg_ref[...]) + gb_ref[...]
    r = jax.nn.sigmoid(gates[:, :LRU_DIM])
    ig = jax.nn.sigmoid(gates[:, LRU_DIM:])
    nl = -lam_ref[...]
    neg_sp = jnp.maximum(nl, 0.0) + jnp.log1p(jnp.exp(-jnp.abs(nl)))
    log_a = -LRU_C * r * neg_sp
    th = jnp.tanh(log_a)
    a_ref[...] = jnp.exp(log_a)
    b_ref[...] = jnp.sqrt(-2.0 * th / (1.0 - th)) * ig * xc

    ns = tt // _SLAB
    rowi = lax.broadcasted_iota(jnp.int32, (_SLAB, LRU_DIM), 0)

    def slab(s, h):
        r0 = pl.multiple_of(((ns - 1 - s) if reverse else s) * _SLAB, _SLAB)
        av = a_ref[pl.ds(r0, _SLAB), :]
        bv = b_ref[pl.ds(r0, _SLAB), :]
        for sh in (1, 2, 4):
            if reverse:
                ok = rowi < _SLAB - sh
                a_sh = pltpu.roll(av, _SLAB - sh, 0)
                b_sh = pltpu.roll(bv, _SLAB - sh, 0)
            else:
                ok = rowi >= sh
                a_sh = pltpu.roll(av, sh, 0)
                b_sh = pltpu.roll(bv, sh, 0)
            bv = jnp.where(ok, av * b_sh + bv, bv)
            av = jnp.where(ok, av * a_sh, av)
        hs = av * h + bv
        hs_ref[pl.ds(r0, _SLAB), :] = hs
        last = hs[0:1] if reverse else hs[_SLAB - 1:_SLAB]
        return jnp.broadcast_to(last, (_SLAB, LRU_DIM))

    hcar_ref[...] = lax.fori_loop(0, ns, slab, hcar_ref[...])
    if combine:
        o_ref[0] = ((hs_ref[...] + hb_ref[0]) * jax.nn.gelu(gt_ref[0])).astype(o_ref.dtype)


def _rglru(xg, h0, conv_w, conv_b, wg, gate_b, lam, reverse, hb=None):
    b, t, _ = xg.shape
    tt = min(256, t)
    nt = t // tt
    combine = hb is not None
    tidx = (lambda i: nt - 1 - i) if reverse else (lambda i: i)
    in_specs = [pl.BlockSpec((1, tt, LRU_DIM), lambda b, i: (b, tidx(i), 0)),
                pl.BlockSpec((1, 1, LRU_DIM), lambda b, i: (b, 0, 0)),
                _const_spec(conv_w.shape), _const_spec((1, LRU_DIM)), _const_spec(wg.shape),
                _const_spec((1, 2 * LRU_DIM)), _const_spec((1, LRU_DIM))]
    args = [xg, h0.reshape(b, 1, LRU_DIM), conv_w, conv_b.reshape(1, -1), wg, gate_b.reshape(1, -1), lam.reshape(1, -1)]
    if combine:
        in_specs += [pl.BlockSpec((1, tt, LRU_DIM), lambda b, i: (b, tidx(i), 0)),
                     pl.BlockSpec((1, tt, LRU_DIM), lambda b, i: (b, tidx(i), 1))]
        args += [hb, xg]
    return pl.pallas_call(
        functools.partial(_lru_kernel, tt=tt, reverse=reverse, combine=combine),
        grid=(b, nt),
        in_specs=in_specs,
        out_specs=pl.BlockSpec((1, tt, LRU_DIM), lambda b, i: (b, tidx(i), 0)),
        out_shape=jax.ShapeDtypeStruct((b, t, LRU_DIM), _BF if combine else _F32),
        scratch_shapes=[pltpu.VMEM((tt + _SLAB, LRU_DIM), _F32),
                        pltpu.VMEM((tt, LRU_DIM), _F32),
                        pltpu.VMEM((tt, LRU_DIM), _F32),
                        pltpu.VMEM((_SLAB, LRU_DIM), _F32),
                        pltpu.VMEM((_SLAB, LRU_DIM), _F32)]
        + ([pltpu.VMEM((tt, LRU_DIM), _F32)] if combine else []),
        compiler_params=_params("parallel", "arbitrary"),
        name="rglru_combine" if combine else ("rglru_rev" if reverse else "rglru_fwd"),
    )(*args)


def _mla_kernel(q_ref, kl_ref, vl_ref, kc_ref, vc_ref, o_ref, m_ref, acc_ref, *, nchunk, tq):
    c = (MLA_NOPE + MLA_ROPE) ** -0.5 * _LOG2E
    m_ref[...] = jnp.full(m_ref.shape, NEG_INF, _F32)
    acc_ref[...] = jnp.zeros(acc_ref.shape, _F32)

    def lanes(hd):
        return slice(hd * LANES, (hd + 1) * LANES)

    def vrows(hd):
        return slice(hd * _MLA_VT, (hd + 1) * _MLA_VT)

    def scores(hd, k):
        return _dot_nt(k, q_ref[0, :, lanes(hd)]) * c

    def update(hd, s, vt):
        m = m_ref[hd]
        m_new = jnp.maximum(m, jnp.max(s, axis=0, keepdims=True))
        p = jnp.exp2(s - m_new).astype(_BF)
        acc_ref[hd] = acc_ref[hd] * jnp.exp2(m - m_new) + _dot(vt, p)
        m_ref[hd] = m_new

    def chunk(keys, values):
        ahead = 2
        s = {hd: scores(hd, keys(hd)) for hd in range(ahead)}
        for hd in range(MLA_HEADS):
            if hd + ahead < MLA_HEADS:
                s[hd + ahead] = scores(hd + ahead, keys(hd + ahead))
            update(hd, s.pop(hd), values(hd))

    def body(ci, carry):
        r0 = pl.multiple_of(ci * _MLA_TK, _MLA_TK)
        chunk(lambda hd: kl_ref[0, pl.ds(r0, _MLA_TK), lanes(hd)], lambda hd: vl_ref[0, ci, vrows(hd), :])
        return carry

    lax.fori_loop(0, nchunk, body, 0)
    chunk(lambda hd: kc_ref[0, :, lanes(hd)], lambda hd: vc_ref[0, 0, vrows(hd), :])
    for pair in range(MLA_HEADS // 2):
        outs = [acc_ref[hd, :MLA_V] / acc_ref[hd, MLA_V:MLA_V + 1] for hd in (2 * pair, 2 * pair + 1)]
        o_ref[0, :, lanes(pair)] = jnp.transpose(jnp.concatenate(outs, axis=0)).astype(_BF)


def _mla_attention(q, k_lat, vt_lat, k_ctx, vt_ctx):
    b, t, _ = q.shape
    lc = k_ctx.shape[1]
    nchunk = vt_lat.shape[1]
    tq = 256
    return pl.pallas_call(
        functools.partial(_mla_kernel, nchunk=nchunk, tq=tq),
        grid=(b, t // tq),
        in_specs=[pl.BlockSpec((1, tq, 1024), lambda b, i: (b, i, 0)),
                  pl.BlockSpec((1, t, 1024), lambda b, i: (b, 0, 0)),
                  pl.BlockSpec((1,) + vt_lat.shape[1:], lambda b, i: (b, 0, 0, 0)),
                  pl.BlockSpec((1, lc, 1024), lambda b, i: (b, 0, 0)),
                  pl.BlockSpec((1,) + vt_ctx.shape[1:], lambda b, i: (b, 0, 0, 0))],
        out_specs=pl.BlockSpec((1, tq, 512), lambda b, i: (b, i, 0)),
        out_shape=jax.ShapeDtypeStruct((b, t, 512), _BF),
        scratch_shapes=[pltpu.VMEM((MLA_HEADS, 1, tq), _F32),
                        pltpu.VMEM((MLA_HEADS, _MLA_VT, tq), _F32)],
        compiler_params=_params("parallel", "parallel"),
        name="mla_attention",
    )(q, k_lat, vt_lat, k_ctx, vt_ctx)


def _rope_partner_cols(w, dim):
    q = dim // 4
    return jnp.concatenate([-w[:, q:2 * q], w[:, :q], -w[:, 3 * q:], w[:, 2 * q:3 * q]], axis=1)


def _rope_tables(t, dim):
    rows = t // GRID_W
    row = jnp.repeat(jnp.arange(rows), GRID_W).astype(_F32)
    col = jnp.tile(jnp.arange(GRID_W), rows).astype(_F32)
    nf = dim // 4
    inv = ROPE_THETA ** (-jnp.arange(nf, dtype=_F32) / nf)
    ar = row[:, None] * inv
    ac = col[:, None] * inv
    ang = jnp.concatenate([ar, ar, ac, ac], axis=-1)
    return jnp.cos(ang), jnp.sin(ang)


def _ab_weights(w_in, w_out):
    hd = WIN_HEAD_DIM
    g = WIN_HEADS // WIN_KV_HEADS
    glu = w_in[:, :1024]
    wq = w_in[:, 1024:1536]
    wk = w_in[:, 1536:1664]
    wv = w_in[:, 1664:1792]
    zeros = jnp.zeros_like(wk[:, :hd])
    qh = [wq[:, h * hd:(h + 1) * hd] for h in range(WIN_HEADS)]
    kh = [wk[:, h * hd:(h + 1) * hd] for h in range(WIN_KV_HEADS)]
    part = lambda w: _rope_partner_cols(w, hd)
    q_full = jnp.concatenate([jnp.concatenate([qh[j], qh[g + j]], 1) for j in range(g)], 1)
    q_part = jnp.concatenate([jnp.concatenate([part(qh[j]), part(qh[g + j])], 1) for j in range(g)], 1)
    k_full = jnp.concatenate([kh[0], zeros, zeros, kh[1]], 1)
    k_part = jnp.concatenate([part(kh[0]), zeros, zeros, part(kh[1])], 1)
    w_ext = jnp.concatenate([glu, q_full, k_full, wv, q_part, k_part], 1).astype(_BF)
    wa = w_out[:CONV_A_DIM].astype(_BF)
    wb_rows = w_out[CONV_A_DIM:]
    wb = jnp.concatenate([jnp.concatenate([wb_rows[j * hd:(j + 1) * hd], wb_rows[(g + j) * hd:(g + j + 1) * hd]], 0)
                          for j in range(g)], 0).astype(_BF)
    return w_ext, wa, wb


def _cd_weights(w_in, w_uq, w_ukv, lat):
    d = w_in.shape[0]
    kr = w_in[:, 1664:1696]
    z32 = jnp.zeros((d, MLA_ROPE), w_in.dtype)
    z64 = jnp.zeros((d, MLA_NOPE), w_in.dtype)
    if lat:
        kr128 = jnp.concatenate([z64, kr, z32], 1)
        krp128 = jnp.concatenate([z64, _rope_partner_cols(kr, MLA_ROPE), z32], 1)
    else:
        kr128 = jnp.concatenate([z64, z32, kr], 1)
        krp128 = jnp.zeros((d, LANES), w_in.dtype)
    w1 = jnp.concatenate([w_in[:, :1664], kr128, krp128], 1).astype(_BF)
    qd = MLA_NOPE + MLA_ROPE
    zq64 = jnp.zeros((MLA_Q_RANK, MLA_NOPE), w_uq.dtype)
    zq32 = jnp.zeros((MLA_Q_RANK, MLA_ROPE), w_uq.dtype)
    full, part = [], []
    for h in range(MLA_HEADS):
        nope = w_uq[:, h * qd:h * qd + MLA_NOPE]
        rope = w_uq[:, h * qd + MLA_NOPE:(h + 1) * qd]
        full += [nope, rope, rope]
        part += [zq64, _rope_partner_cols(rope, MLA_ROPE), zq32]
    wq = jnp.concatenate(full + part, 1).astype(_BF)
    kd = MLA_NOPE + MLA_V
    zk = jnp.zeros((MLA_KV_RANK, LANES - MLA_NOPE), w_ukv.dtype)
    kcols, vcols = [], []
    for h in range(MLA_HEADS):
        kcols += [w_ukv[:, h * kd:h * kd + MLA_NOPE], zk]
        vcols += [w_ukv[:, h * kd + MLA_NOPE:(h + 1) * kd]]
    wkv = jnp.concatenate(kcols + vcols, 1).astype(_BF)
    return w1, wq, wkv


def _ffn_weights(w_up, conv_w, conv_b, w_down):
    d = w_up.shape[0]
    wup = w_up.reshape(d, 2 * FFN_NCHUNK, FFN_CHUNK).transpose(1, 0, 2).astype(_BF)
    cw = conv_w.reshape(3, FFN_NCHUNK, FFN_CHUNK).transpose(1, 0, 2)
    cb = conv_b.reshape(FFN_NCHUNK, 1, FFN_CHUNK)
    wdn = w_down.reshape(FFN_NCHUNK, FFN_CHUNK, d).astype(_BF)
    return wup, cw, cb, wdn


def _block_diag_dense(w):
    nb, hh, kk = w.shape
    eye = jnp.eye(nb, dtype=w.dtype)
    return (eye[:, None, :, None] * w[:, :, None, :]).reshape(nb * hh, nb * kk)


def kernel(x, c, ctx, c_ctx, w_mod, b_mod, norm_g, ffn_w_up, ffn_conv_w, ffn_conv_b, ffn_w_down, ab_w_in, a_conv_w, a_conv_b, a_ln_g, a_ln_b, b_sink, ab_w_out, cd_w_in, lru_conv_w, lru_conv_b, lru_gate_w, lru_gate_b, lru_lambda, mla_q_norm, mla_w_uq, mla_kv_norm, mla_w_ukv, cd_w_out):
    b, t, d = x.shape
    lc = ctx.shape[1]
    ctx_row = 8 * ((b + 7) // 8)
    cc = jnp.zeros((ctx_row + 8, d), _F32).at[:b].set(c).at[ctx_row].set(c_ctx)
    mods = _modulation(cc, w_mod, b_mod).reshape(w_mod.shape[0], ctx_row + 8, N_MOD, d)

    mod, g = mods[0], norm_g[0]
    w_ext, wa, wb = _ab_weights(ab_w_in[0], ab_w_out[0])
    cos64, sin64 = _rope_tables(t, WIN_HEAD_DIM)
    cos_l = jnp.concatenate([cos64, cos64], 1)
    sin_l = jnp.concatenate([sin64, sin64], 1)
    cos_c = jnp.ones((lc, LANES), _F32)
    sin_c = jnp.zeros((lc, LANES), _F32)
    glu_l, q_l, k_l, v_l = _ab_in(x, mod, None, g, w_ext, cos_l, sin_l)
    glu_c, q_c, k_c, v_c = _ab_in(ctx, mod, ctx_row, g, w_ext, cos_c, sin_c)
    a_l = _conformer_conv(glu_l, a_conv_w[0], a_conv_b[0], a_ln_g[0], a_ln_b[0])
    a_c = _conformer_conv(glu_c, a_conv_w[0], a_conv_b[0], a_ln_g[0], a_ln_b[0])
    b_l = _win_attention(q_l, k_l, v_l, k_c, v_c, b_sink[0], True)
    b_c = _win_attention(q_c, k_c, v_c, k_c, v_c, b_sink[0], False)
    xl, hl = _mixer_out(a_l, b_l, wa, wb, x, mod, None, g)
    xc, hc = _mixer_out(a_c, b_c, wa, wb, ctx, mod, ctx_row, g)
    fw = _ffn_weights(ffn_w_up[0], ffn_conv_w[0], ffn_conv_b[0], ffn_w_down[0])
    xl = _ffn(hl, xl, *fw, mod, None, g)
    xc = _ffn(hc, xc, *fw, mod, ctx_row, g)

    mod, g = mods[1], norm_g[1]
    w1_l, wq, wkv = _cd_weights(cd_w_in[0], mla_w_uq[0], mla_w_ukv[0], True)
    w1_c, _, _ = _cd_weights(cd_w_in[0], mla_w_uq[0], mla_w_ukv[0], False)
    cos32, sin32 = _rope_tables(t, MLA_ROPE)
    one64, zero64 = jnp.ones((t, MLA_NOPE), _F32), jnp.zeros((t, MLA_NOPE), _F32)
    one32, zero32 = jnp.ones((t, MLA_ROPE), _F32), jnp.zeros((t, MLA_ROPE), _F32)
    cos_t = jnp.concatenate([one64, cos32, one32], 1)
    sin_t = jnp.concatenate([zero64, sin32, zero32], 1)
    qn = mla_q_norm[0].reshape(1, -1)
    kvn = mla_kv_norm[0].reshape(1, -1)
    xg_l, q_l, k_l, v_l = _cd_in(xl, mod, None, g, w1_l, qn, wq, kvn, wkv, cos_t, sin_t, cos_t, sin_t)
    xg_c, _, k_c, v_c = _cd_in(xc, mod, ctx_row, g, w1_c, qn, wq, kvn, wkv, cos_c, sin_c, cos_c, sin_c)
    wg = [jnp.concatenate([_block_diag_dense(lru_gate_w[0, dr, 0]), _block_diag_dense(lru_gate_w[0, dr, 1])], 1).astype(_BF)
          for dr in range(2)]
    gb = [jnp.concatenate([lru_gate_b[0, dr, 0], lru_gate_b[0, dr, 1]]) for dr in range(2)]
    lru = lambda xg, h0, dr, hb=None: _rglru(xg, h0, lru_conv_w[0, dr], lru_conv_b[0, dr], wg[dr], gb[dr],
                                             lru_lambda[0, dr], dr == 1, hb)
    h_zero = jnp.zeros((b, LRU_DIM), _F32)
    hf_c = lru(xg_c, h_zero, 0)
    hb_c = lru(xg_c, h_zero, 1)
    hb_l = lru(xg_l, hb_c[:, 0], 1)
    c_l = lru(xg_l, hf_c[:, lc - 1], 0, hb_l)
    d_l = _mla_attention(q_l, k_l, v_l, k_c, v_c)
    wo = cd_w_out[0]
    xl, hl = _mixer_out(c_l, d_l, wo[:LRU_DIM].astype(_BF), wo[LRU_DIM:].astype(_BF), xl, mod, None, g)
    fw = _ffn_weights(ffn_w_up[1], ffn_conv_w[1], ffn_conv_b[1], ffn_w_down[1])
    return _ffn(hl, xl, *fw, mod, None, g)
```

```python
import functools

import jax
import jax.numpy as jnp
from jax import lax
from jax.experimental import pallas as pl
from jax.experimental.pallas import tpu as pltpu

_BF = jnp.bfloat16
_F32 = jnp.float32

D_MODEL = 1024
N_MOD = 6
EPS = 1e-6
ROPE_THETA = 10000.0
GRID_W = 64
NEG_INF = -1e30

CONV_A_DIM = 512
CONV_A_WIDTH = 31
WIN_HEADS = 8
WIN_KV_HEADS = 2
WIN_HEAD_DIM = 64
WINDOW = 128
LRU_DIM = 512
LRU_BLOCKS = 8
LRU_CONV_WIDTH = 4
LRU_C = 8.0
MLA_HEADS = 8
MLA_Q_RANK = 384
MLA_KV_RANK = 256
MLA_NOPE = 64
MLA_ROPE = 32
MLA_V = 64
FFN_DIM = 2816
FFN_CHUNK = 256
FFN_NCHUNK = FFN_DIM // FFN_CHUNK
LANES = 128
_SLAB = 8
HALO = 16
VMEM_LIMIT = 56 * 1024 * 1024
_LOG2E = 1.4426950408889634
_MLA_TK = 512
_MLA_VT = 80
_MLA_SUB = 512
_MLA_AHEAD = 4
_WIN_VT = 80
_WIN_QSCALE = WIN_HEAD_DIM ** -0.5 * _LOG2E
_WIN_AHEAD = 3
_MLA_UNROLL = 4


def _params(*sem):
    return pltpu.CompilerParams(dimension_semantics=sem, vmem_limit_bytes=VMEM_LIMIT)


def _const_spec(shape):
    nd = len(shape)
    return pl.BlockSpec(shape, lambda *_: (0,) * nd, pipeline_mode=pl.Buffered(1))


def _rms(x, g):
    return x * lax.rsqrt(jnp.mean(x * x, axis=-1, keepdims=True) + EPS) * g


def _dot(a, b):
    return jnp.dot(a, b, preferred_element_type=_F32)


def _dot_nt(a, b):
    return lax.dot_general(a, b, (((1,), (1,)), ((), ())), preferred_element_type=_F32)


def _mod_spec(mod_row):
    if mod_row is None:
        return pl.BlockSpec((1, N_MOD, D_MODEL), lambda b, i: (b, 0, 0))
    return pl.BlockSpec((1, N_MOD, D_MODEL), lambda b, i: (mod_row, 0, 0))


def _mod_kernel(c_ref, w_ref, b_ref, o_ref):
    c = c_ref[...]
    s = (c * jax.nn.sigmoid(c)).astype(_BF)
    o_ref[0] = _dot(s, w_ref[0].astype(_BF)) + b_ref[0]


def _modulation(cc, w_mod, b_mod):
    depth, d, n = w_mod.shape
    tn = 512
    return pl.pallas_call(
        _mod_kernel,
        grid=(depth, n // tn),
        in_specs=[pl.BlockSpec(cc.shape, lambda l, j: (0, 0)),
                  pl.BlockSpec((1, d, tn), lambda l, j: (l, 0, j)),
                  pl.BlockSpec((1, 1, tn), lambda l, j: (l, 0, j))],
        out_specs=pl.BlockSpec((1, cc.shape[0], tn), lambda l, j: (l, 0, j)),
        out_shape=jax.ShapeDtypeStruct((depth, cc.shape[0], n), _F32),
        compiler_params=_params("parallel", "parallel"),
        name="modulation",
    )(cc, w_mod, b_mod.reshape(depth, 1, n))


_AB_Q0, _AB_K0, _AB_V0, _AB_QP0, _AB_KP0, _AB_N = 1024, 1536, 1792, 1920, 2432, 2688


def _ab_in_kernel(x_ref, mod_ref, g_ref, w_ref, cq_ref, sq_ref, ck_ref, sk_ref, glu_ref, q_ref, k_ref, vt_ref):
    h = (_rms(x_ref[0], g_ref[0:1]) * (1.0 + mod_ref[0, 1:2]) + mod_ref[0, 0:1]).astype(_BF)
    z = _dot(h, w_ref[...])
    glu_ref[0] = z[:, :_AB_Q0]
    cosq = cq_ref[...]
    sinq = sq_ref[...]
    for j in range(4):
        qf = z[:, _AB_Q0 + LANES * j:_AB_Q0 + LANES * (j + 1)]
        qp = z[:, _AB_QP0 + LANES * j:_AB_QP0 + LANES * (j + 1)]
        q_ref[0, j] = (qf * cosq + qp * sinq).astype(_BF)
        q_ref[0, 4 + j] = (qf * _WIN_QSCALE).astype(_BF)
    for j in range(2):
        kf = z[:, _AB_K0 + LANES * j:_AB_K0 + LANES * (j + 1)]
        kp = z[:, _AB_KP0 + LANES * j:_AB_KP0 + LANES * (j + 1)]
        k_ref[0, :, LANES * j:LANES * (j + 1)] = (kf * ck_ref[...] + kp * sk_ref[...]).astype(_BF)
    vt = jnp.transpose(z[:, _AB_V0:_AB_QP0]).astype(_BF)
    ones = jnp.ones((_WIN_VT - WIN_HEAD_DIM, LANES), _BF)
    for cb in range(vt.shape[1] // LANES):
        for kv in range(WIN_KV_HEADS):
            vt_ref[0, cb, kv * _WIN_VT:kv * _WIN_VT + WIN_HEAD_DIM, :] = (
                vt[kv * WIN_HEAD_DIM:(kv + 1) * WIN_HEAD_DIM, cb * LANES:(cb + 1) * LANES])
            vt_ref[0, cb, kv * _WIN_VT + WIN_HEAD_DIM:(kv + 1) * _WIN_VT, :] = ones


def _ab_in(x, mod, mod_row, g, w_ext, cosq, sinq, cosk, sink):
    b, t, d = x.shape
    tm = min(512, t)
    tab = pl.BlockSpec((tm, LANES), lambda b, i: (i, 0))
    return pl.pallas_call(
        _ab_in_kernel,
        grid=(b, t // tm),
        in_specs=[pl.BlockSpec((1, tm, d), lambda b, i: (b, i, 0)),
                  _mod_spec(mod_row),
                  _const_spec(g.shape),
                  _const_spec(w_ext.shape),
                  tab, tab, tab, tab],
        out_specs=[pl.BlockSpec((1, tm, 1024), lambda b, i: (b, i, 0)),
                   pl.BlockSpec((1, 8, tm, LANES), lambda b, i: (b, 0, i, 0)),
                   pl.BlockSpec((1, tm, 256), lambda b, i: (b, i, 0)),
                   pl.BlockSpec((1, tm // LANES, WIN_KV_HEADS * _WIN_VT, LANES), lambda b, i: (b, i, 0, 0))],
        out_shape=[jax.ShapeDtypeStruct((b, t, 1024), _F32),
                   jax.ShapeDtypeStruct((b, 8, t, LANES), _BF),
                   jax.ShapeDtypeStruct((b, t, 256), _BF),
                   jax.ShapeDtypeStruct((b, t // LANES, WIN_KV_HEADS * _WIN_VT, LANES), _BF)],
        compiler_params=_params("parallel", "parallel"),
        name="ab_in",
    )(x, mod, g, w_ext, cosq, sinq, cosk, sink)


_CONV_ROWS = 64


def _conv_kernel(glu_ref, prev_ref, next_ref, w_ref, cb_ref, lg_ref, lb_ref, o_ref, uext_ref, *, tt, nt):
    i = pl.program_id(1)

    def glu(z):
        return z[:, :CONV_A_DIM] * jax.nn.sigmoid(z[:, CONV_A_DIM:])

    uext_ref[0, 0:HALO] = jnp.where(i > 0, glu(prev_ref[0]), 0.0)
    uext_ref[0, HALO:HALO + tt] = glu(glu_ref[0])
    uext_ref[0, HALO + tt:2 * HALO + tt] = jnp.where(i < nt - 1, glu(next_ref[0]), 0.0)
    span = tt + 2 * HALO - _SLAB
    for p in range(1, _SLAB):
        uext_ref[p, 0:span] = uext_ref[0, p:p + span]
    half = CONV_A_WIDTH // 2
    for r in range(tt // _CONV_ROWS):
        acc = jnp.zeros((_CONV_ROWS, CONV_A_DIM), _F32)
        for j in range(CONV_A_WIDTH):
            off = r * _CONV_ROWS + HALO - half + j
            p = off % _SLAB
            acc = acc + w_ref[j:j + 1, :] * uext_ref[p, off - p:off - p + _CONV_ROWS, :]
        u = acc + cb_ref[...]
        mu = jnp.mean(u, axis=-1, keepdims=True)
        var = jnp.mean(jnp.square(u - mu), axis=-1, keepdims=True)
        y = (u - mu) * lax.rsqrt(var + EPS) * lg_ref[...] + lb_ref[...]
        o_ref[0, r * _CONV_ROWS:(r + 1) * _CONV_ROWS, :] = (y * jax.nn.sigmoid(y)).astype(_BF)


def _conformer_conv(glu, conv_w, conv_b, ln_g, ln_b):
    b, t, _ = glu.shape
    tt = min(512, t)
    nt = t // tt
    hb = tt // HALO
    return pl.pallas_call(
        functools.partial(_conv_kernel, tt=tt, nt=nt),
        grid=(b, nt),
        in_specs=[pl.BlockSpec((1, tt, 1024), lambda b, i: (b, i, 0)),
                  pl.BlockSpec((1, HALO, 1024), lambda b, i: (b, jnp.maximum(i * hb - 1, 0), 0)),
                  pl.BlockSpec((1, HALO, 1024), lambda b, i: (b, jnp.minimum((i + 1) * hb, t // HALO - 1), 0)),
                  _const_spec(conv_w.shape),
                  _const_spec((1, CONV_A_DIM)), _const_spec((1, CONV_A_DIM)), _const_spec((1, CONV_A_DIM))],
        out_specs=pl.BlockSpec((1, tt, CONV_A_DIM), lambda b, i: (b, i, 0)),
        out_shape=jax.ShapeDtypeStruct((b, t, CONV_A_DIM), _BF),
        scratch_shapes=[pltpu.VMEM((_SLAB, tt + 2 * HALO, CONV_A_DIM), _F32)],
        compiler_params=_params("parallel", "parallel"),
        name="conformer_conv",
    )(glu, glu, glu, conv_w, conv_b.reshape(1, -1), ln_g.reshape(1, -1), ln_b.reshape(1, -1))


def _win_kernel(sink_ref, q_ref, kl_ref, vl_ref, kc_ref, vc_ref, o_ref, *, t, tq, has_lat):
    n = pl.program_id(1)
    g = WIN_HEADS // WIN_KV_HEADS
    nctx = kc_ref.shape[1] // LANES
    if has_lat:
        band = tq + 2 * WINDOW
        start = pl.multiple_of(jnp.clip(n * tq - WINDOW, 0, t - band), LANES)
        kpos = start + lax.broadcasted_iota(jnp.int32, (band, 1), 0)
        qpos = n * tq + lax.broadcasted_iota(jnp.int32, (1, tq), 1)
        bias = jnp.where(jnp.abs(qpos - kpos) <= WINDOW, 0.0, NEG_INF)

    def kcols(kv):
        return slice(kv * LANES, (kv + 1) * LANES)

    def vrows(kv):
        return slice(kv * _WIN_VT, (kv + 1) * _WIN_VT)

    def scores(kv, j):
        s_ctx = _dot_nt(kc_ref[0, :, kcols(kv)], q_ref[0, (g + j) if has_lat else j])
        if not has_lat:
            return (s_ctx,)
        return s_ctx, _dot_nt(kl_ref[0, pl.ds(start, band), kcols(kv)], q_ref[0, j]) + bias

    def finish(kv, j, s):
        t_sink = jnp.full((1, tq), sink_ref[kv * g + j] * _LOG2E, _F32)
        m = t_sink
        for part in s:
            m = jnp.maximum(m, jnp.max(part, axis=0, keepdims=True))
        vt_ctx = jnp.concatenate([vc_ref[0, cb, vrows(kv), :] for cb in range(nctx)], axis=1)
        acc = _dot(vt_ctx, jnp.exp2(s[0] - m).astype(_BF))
        if has_lat:
            sb = start // LANES
            vt_lat = jnp.concatenate([vl_ref[0, sb + cb, vrows(kv), :] for cb in range(band // LANES)], axis=1)
            acc = acc + _dot(vt_lat, jnp.exp2(s[1] - m).astype(_BF))
        den = acc[WIN_HEAD_DIM:WIN_HEAD_DIM + 1] + jnp.exp2(t_sink - m)
        return acc[:WIN_HEAD_DIM] / den

    units = [(kv, j) for j in range(g) for kv in range(WIN_KV_HEADS)]
    s = {u: scores(*units[u]) for u in range(min(_WIN_AHEAD, len(units)))}
    outs = {}
    for u, (kv, j) in enumerate(units):
        nxt = u + _WIN_AHEAD
        if nxt < len(units):
            s[nxt] = scores(*units[nxt])
        outs[(kv, j)] = finish(kv, j, s.pop(u))
        if kv == WIN_KV_HEADS - 1:
            both = jnp.concatenate([outs.pop((k2, j)) for k2 in range(WIN_KV_HEADS)], axis=0)
            o_ref[0, :, j * LANES:(j + 1) * LANES] = jnp.transpose(both).astype(_BF)


def _win_attention(q, k_lat, vt_lat, k_ctx, vt_ctx, sink, has_lat):
    b, _, t, _ = q.shape
    lc = k_ctx.shape[1]
    tq = 256 if has_lat else t
    tl = k_lat.shape[1]
    return pl.pallas_call(
        functools.partial(_win_kernel, t=t, tq=tq, has_lat=has_lat),
        grid=(b, t // tq),
        in_specs=[pl.BlockSpec(memory_space=pltpu.SMEM),
                  pl.BlockSpec((1, 8, tq, LANES), lambda b, n: (b, 0, n, 0)),
                  pl.BlockSpec((1, tl, 256), lambda b, n: (b, 0, 0)),
                  pl.BlockSpec((1,) + vt_lat.shape[1:], lambda b, n: (b, 0, 0, 0)),
                  pl.BlockSpec((1, lc, 256), lambda b, n: (b, 0, 0)),
                  pl.BlockSpec((1,) + vt_ctx.shape[1:], lambda b, n: (b, 0, 0, 0))],
        out_specs=pl.BlockSpec((1, tq, 512), lambda b, n: (b, n, 0)),
        out_shape=jax.ShapeDtypeStruct((b, t, 512), _BF),
        compiler_params=_params("parallel", "parallel"),
        name="win_attention" if has_lat else "ctx_attention",
    )(sink, q, k_lat, vt_lat, k_ctx, vt_ctx)


def _out_kernel(a_ref, b_ref, wa_ref, wb_ref, x_ref, mod_ref, g_ref, xo_ref, h_ref):
    y = _dot(a_ref[0], wa_ref[...]) + _dot(b_ref[0], wb_ref[...])
    x1 = x_ref[0] + mod_ref[0, 2:3] * _rms(y, g_ref[1:2])
    xo_ref[0] = x1
    h_ref[0] = (_rms(x1, g_ref[2:3]) * (1.0 + mod_ref[0, 4:5]) + mod_ref[0, 3:4]).astype(_BF)


def _mixer_out(a, bb, wa, wb, x, mod, mod_row, g):
    b, t, d = x.shape
    tm = min(512, t)
    return pl.pallas_call(
        _out_kernel,
        grid=(b, t // tm),
        in_specs=[pl.BlockSpec((1, tm, 512), lambda b, i: (b, i, 0)),
                  pl.BlockSpec((1, tm, 512), lambda b, i: (b, i, 0)),
                  _const_spec(wa.shape), _const_spec(wb.shape),
                  pl.BlockSpec((1, tm, d), lambda b, i: (b, i, 0)),
                  _mod_spec(mod_row),
                  _const_spec(g.shape)],
        out_specs=[pl.BlockSpec((1, tm, d), lambda b, i: (b, i, 0)),
                   pl.BlockSpec((1, tm, d), lambda b, i: (b, i, 0))],
        out_shape=[jax.ShapeDtypeStruct((b, t, d), _F32),
                   jax.ShapeDtypeStruct((b, t, d), _BF)],
        compiler_params=_params("parallel", "parallel"),
        name="mixer_out",
    )(a, bb, wa, wb, x, mod, g)


def _ffn_kernel(h_ref, hp_ref, hn_ref, x_ref, wup_ref, cw_ref, cb_ref, wdn_ref, mod_ref, g_ref,
                o_ref, gbuf_ref, acc_ref, *, tm, nt):
    i = pl.program_id(1)
    cur = h_ref[0]
    lhs = jnp.concatenate([hp_ref[0], cur, hn_ref[0]], axis=0)
    row = lax.broadcasted_iota(jnp.int32, (tm + 2 * HALO, 1), 0)
    keep = jnp.logical_and(jnp.logical_or(i > 0, row >= HALO), jnp.logical_or(i < nt - 1, row < HALO + tm))

    def up(c):
        return jnp.where(keep, _dot(lhs, wup_ref[c]), 0.0), _dot(cur, wup_ref[FFN_NCHUNK + c])

    def down(c, g, u):
        buf = gbuf_ref.at[c % 2]
        buf[...] = g
        w = cw_ref[c]
        gc = (w[0:1] * buf[HALO - 1:HALO - 1 + tm, :] + w[1:2] * buf[HALO:HALO + tm, :]
              + w[2:3] * buf[HALO + 1:HALO + 1 + tm, :] + cb_ref[c])
        y = _dot((jax.nn.gelu(gc) * u).astype(_BF), wdn_ref[c])
        if c == 0:
            acc_ref[...] = y
        else:
            acc_ref[...] += y

    pending = up(0)
    for c in range(FFN_NCHUNK):
        following = up(c + 1) if c + 1 < FFN_NCHUNK else None
        down(c, *pending)
        pending = following
    o_ref[0] = x_ref[0] + mod_ref[0, 5:6] * _rms(acc_ref[...], g_ref[3:4])


def _ffn(h, x, wup, cw, cb, wdn, mod, mod_row, g):
    b, t, d = x.shape
    tm = min(512, t)
    nt = t // tm
    hb = tm // HALO
    return pl.pallas_call(
        functools.partial(_ffn_kernel, tm=tm, nt=nt),
        grid=(b, nt),
        in_specs=[pl.BlockSpec((1, tm, d), lambda b, i: (b, i, 0)),
                  pl.BlockSpec((1, HALO, d), lambda b, i: (b, jnp.maximum(i * hb - 1, 0), 0)),
                  pl.BlockSpec((1, HALO, d), lambda b, i: (b, jnp.minimum((i + 1) * hb, t // HALO - 1), 0)),
                  pl.BlockSpec((1, tm, d), lambda b, i: (b, i, 0)),
                  _const_spec(wup.shape), _const_spec(cw.shape), _const_spec(cb.shape), _const_spec(wdn.shape),
                  _mod_spec(mod_row),
                  _const_spec(g.shape)],
        out_specs=pl.BlockSpec((1, tm, d), lambda b, i: (b, i, 0)),
        out_shape=jax.ShapeDtypeStruct((b, t, d), _F32),
        scratch_shapes=[pltpu.VMEM((2, tm + 2 * HALO, FFN_CHUNK), _F32),
                        pltpu.VMEM((tm, d), _F32)],
        compiler_params=_params("parallel", "parallel"),
        name="conv_glu_ffn",
    )(h, h, h, x, wup, cw, cb, wdn, mod, g)


_CD_CQ0, _CD_CKV0, _CD_KR0, _CD_KRP0, _CD_N = 1024, 1408, 1664, 1792, 1920


def _cd_in_kernel(x_ref, mod_ref, g_ref, w1_ref, qn_ref, wq_ref, kvn_ref, wkv_ref,
                  cq_ref, sq_ref, ck_ref, sk_ref, xg_ref, q_ref, k_ref, v_ref):
    h = (_rms(x_ref[0], g_ref[0:1]) * (1.0 + mod_ref[0, 1:2]) + mod_ref[0, 0:1]).astype(_BF)
    z = _dot(h, w1_ref[...])
    xg_ref[0] = z[:, :_CD_CQ0]
    cqn = _rms(z[:, _CD_CQ0:_CD_CKV0], qn_ref[...]).astype(_BF)
    qq = _dot(cqn, wq_ref[...])
    cosq = cq_ref[...]
    sinq = sq_ref[...]
    nq = MLA_HEADS * LANES
    for hd in range(MLA_HEADS):
        lo, hi = hd * LANES, (hd + 1) * LANES
        q_ref[0, :, lo:hi] = (qq[:, lo:hi] * cosq + qq[:, nq + lo:nq + hi] * sinq).astype(_BF)
    ckvn = _rms(z[:, _CD_CKV0:_CD_KR0], kvn_ref[...]).astype(_BF)
    kv = _dot(ckvn, wkv_ref[...])
    krr = z[:, _CD_KR0:_CD_KRP0] * ck_ref[...] + z[:, _CD_KRP0:_CD_N] * sk_ref[...]
    for hd in range(MLA_HEADS):
        lo, hi = hd * LANES, (hd + 1) * LANES
        k_ref[0, :, lo:hi] = (kv[:, lo:hi] + krr).astype(_BF)
    vt = jnp.transpose(kv[:, nq:]).astype(_BF)
    ones = jnp.ones((_MLA_VT - MLA_V, vt.shape[1]), _BF)
    for hd in range(MLA_HEADS):
        v_ref[0, 0, hd * _MLA_VT:hd * _MLA_VT + MLA_V, :] = vt[hd * MLA_V:(hd + 1) * MLA_V]
        v_ref[0, 0, hd * _MLA_VT + MLA_V:(hd + 1) * _MLA_VT, :] = ones


def _cd_in(x, mod, mod_row, g, w1, qn, wq, kvn, wkv, cosq, sinq, cosk, sink):
    b, t, d = x.shape
    tm = min(512, t)
    tab = pl.BlockSpec((tm, LANES), lambda b, i: (i, 0))
    return pl.pallas_call(
        _cd_in_kernel,
        grid=(b, t // tm),
        in_specs=[pl.BlockSpec((1, tm, d), lambda b, i: (b, i, 0)),
                  _mod_spec(mod_row),
                  _const_spec(g.shape), _const_spec(w1.shape), _const_spec(qn.shape), _const_spec(wq.shape),
                  _const_spec(kvn.shape), _const_spec(wkv.shape), tab, tab, tab, tab],
        out_specs=[pl.BlockSpec((1, tm, 1024), lambda b, i: (b, i, 0)),
                   pl.BlockSpec((1, tm, 1024), lambda b, i: (b, i, 0)),
                   pl.BlockSpec((1, tm, 1024), lambda b, i: (b, i, 0)),
                   pl.BlockSpec((1, 1, MLA_HEADS * _MLA_VT, tm), lambda b, i: (b, i, 0, 0))],
        out_shape=[jax.ShapeDtypeStruct((b, t, 1024), _F32),
                   jax.ShapeDtypeStruct((b, t, 1024), _BF),
                   jax.ShapeDtypeStruct((b, t, 1024), _BF),
                   jax.ShapeDtypeStruct((b, t // tm, MLA_HEADS * _MLA_VT, tm), _BF)],
        compiler_params=_params("parallel", "parallel"),
        name="cd_in",
    )(x, mod, g, w1, qn, wq, kvn, wkv, cosq, sinq, cosk, sink)


def _lru_kernel(*refs, tt, reverse, combine):
    if combine:
        (x_ref, h0_ref, cw_ref, cb_ref, wg_ref, gb_ref, lam_ref, hb_ref, gt_ref,
         o_ref, xext_ref, a_ref, b_ref, hcar_ref, xcar_ref, hs_ref) = refs
    else:
        (x_ref, h0_ref, cw_ref, cb_ref, wg_ref, gb_ref, lam_ref,
         o_ref, xext_ref, a_ref, b_ref, hcar_ref, xcar_ref) = refs
        hs_ref = o_ref.at[0]
    i = pl.program_id(1)

    @pl.when(i == 0)
    def _():
        hcar_ref[...] = jnp.broadcast_to(h0_ref[0], (_SLAB, LRU_DIM))
        xcar_ref[...] = jnp.zeros_like(xcar_ref)

    x = x_ref[0]
    if reverse:
        xext_ref[0:tt] = x
        xext_ref[tt:tt + _SLAB] = xcar_ref[...]
        xcar_ref[...] = x[0:_SLAB]
        base = 0
    else:
        xext_ref[0:_SLAB] = xcar_ref[...]
        xext_ref[_SLAB:_SLAB + tt] = x
        xcar_ref[...] = x[tt - _SLAB:tt]
        base = _SLAB - (LRU_CONV_WIDTH - 1)
    xc = cb_ref[...]
    for j in range(LRU_CONV_WIDTH):
        xc = xc + cw_ref[j:j + 1, :] * xext_ref[base + j:base + j + tt, :]
    gates = _dot(xc.astype(_BF), wg_ref[...]) + gb_ref[...]
    r = jax.nn.sigmoid(gates[:, :LRU_DIM])
    ig = jax.nn.sigmoid(gates[:, LRU_DIM:])
    nl = -lam_ref[...]
    neg_sp = jnp.maximum(nl, 0.0) + jnp.log1p(jnp.exp(-jnp.abs(nl)))
    log_a = -LRU_C * r * neg_sp
    th = jnp.tanh(log_a)
    a_ref[...] = jnp.exp(log_a)
    b_ref[...] = jnp.sqrt(-2.0 * th / (1.0 - th)) * ig * xc

    ns = tt // _SLAB
    rowi = lax.broadcasted_iota(jnp.int32, (_SLAB, LRU_DIM), 0)

    def slab(s, h):
        r0 = pl.multiple_of(((ns - 1 - s) if reverse else s) * _SLAB, _SLAB)
        av = a_ref[pl.ds(r0, _SLAB), :]
        bv = b_ref[pl.ds(r0, _SLAB), :]
        for sh in (1, 2, 4):
            if reverse:
                ok = rowi < _SLAB - sh
                a_sh = pltpu.roll(av, _SLAB - sh, 0)
                b_sh = pltpu.roll(bv, _SLAB - sh, 0)
            else:
                ok = rowi >= sh
                a_sh = pltpu.roll(av, sh, 0)
                b_sh = pltpu.roll(bv, sh, 0)
            bv = jnp.where(ok, av * b_sh + bv, bv)
            av = jnp.where(ok, av * a_sh, av)
        hs = av * h + bv
        hs_ref[pl.ds(r0, _SLAB), :] = hs
        last = hs[0:1] if reverse else hs[_SLAB - 1:_SLAB]
        return jnp.broadcast_to(last, (_SLAB, LRU_DIM))

    hcar_ref[...] = lax.fori_loop(0, ns, slab, hcar_ref[...])
    if combine:
        o_ref[0] = ((hs_ref[...] + hb_ref[0]) * jax.nn.gelu(gt_ref[0])).astype(o_ref.dtype)


def _rglru(xg, h0, conv_w, conv_b, wg, gate_b, lam, reverse, hb=None):
    b, t, _ = xg.shape
    tt = min(256, t)
    nt = t // tt
    combine = hb is not None
    tidx = (lambda i: nt - 1 - i) if reverse else (lambda i: i)
    in_specs = [pl.BlockSpec((1, tt, LRU_DIM), lambda b, i: (b, tidx(i), 0)),
                pl.BlockSpec((1, 1, LRU_DIM), lambda b, i: (b, 0, 0)),
                _const_spec(conv_w.shape), _const_spec((1, LRU_DIM)), _const_spec(wg.shape),
                _const_spec((1, 2 * LRU_DIM)), _const_spec((1, LRU_DIM))]
    args = [xg, h0.reshape(b, 1, LRU_DIM), conv_w, conv_b.reshape(1, -1), wg, gate_b.reshape(1, -1), lam.reshape(1, -1)]
    if combine:
        in_specs += [pl.BlockSpec((1, tt, LRU_DIM), lambda b, i: (b, tidx(i), 0)),
                     pl.BlockSpec((1, tt, LRU_DIM), lambda b, i: (b, tidx(i), 1))]
        args += [hb, xg]
    return pl.pallas_call(
        functools.partial(_lru_kernel, tt=tt, reverse=reverse, combine=combine),
        grid=(b, nt),
        in_specs=in_specs,
        out_specs=pl.BlockSpec((1, tt, LRU_DIM), lambda b, i: (b, tidx(i), 0)),
        out_shape=jax.ShapeDtypeStruct((b, t, LRU_DIM), _BF if combine else _F32),
        scratch_shapes=[pltpu.VMEM((tt + _SLAB, LRU_DIM), _F32),
                        pltpu.VMEM((tt, LRU_DIM), _F32),
                        pltpu.VMEM((tt, LRU_DIM), _F32),
                        pltpu.VMEM((_SLAB, LRU_DIM), _F32),
                        pltpu.VMEM((_SLAB, LRU_DIM), _F32)]
        + ([pltpu.VMEM((tt, LRU_DIM), _F32)] if combine else []),
        compiler_params=_params("parallel", "arbitrary"),
        name="rglru_combine" if combine else ("rglru_rev" if reverse else "rglru_fwd"),
    )(*args)


def _mla_kernel(q_ref, kl_ref, vl_ref, kc_ref, vc_ref, o_ref, m_ref, acc_ref, *, nchunk, tq):
    m_ref[...] = jnp.full(m_ref.shape, NEG_INF, _F32)
    acc_ref[...] = jnp.zeros(acc_ref.shape, _F32)

    def lanes(hd):
        return slice(hd * LANES, (hd + 1) * LANES)

    def vrows(hd):
        return slice(hd * _MLA_VT, (hd + 1) * _MLA_VT)

    def scores(hd, k):
        return _dot_nt(k, q_ref[0, :, lanes(hd)])

    def update(hd, s, vt):
        m = m_ref[hd]
        m_new = jnp.maximum(m, jnp.max(s, axis=0, keepdims=True))
        p = jnp.exp2(s - m_new).astype(_BF)
        acc_ref[hd] = acc_ref[hd] * jnp.exp2(m - m_new) + _dot(vt, p)
        m_ref[hd] = m_new

    def run(units):
        s = {j: scores(units[j][0], units[j][1]()) for j in range(min(_MLA_AHEAD, len(units)))}
        for j, (hd, _, values) in enumerate(units):
            nxt = j + _MLA_AHEAD
            if nxt < len(units):
                s[nxt] = scores(units[nxt][0], units[nxt][1]())
            update(hd, s.pop(j), values())

    unroll = max(u for u in range(1, _MLA_UNROLL + 1) if nchunk % u == 0)

    def latent_units(it):
        units = []
        for sub in range(unroll):
            ci = it * unroll + sub
            r0 = ci * _MLA_TK if isinstance(ci, int) else pl.multiple_of(ci * _MLA_TK, _MLA_TK)
            for c0 in range(0, _MLA_TK, _MLA_SUB):
                units += [(hd, functools.partial(lambda hd, r: kl_ref[0, pl.ds(r, _MLA_SUB), lanes(hd)], hd, r0 + c0),
                           functools.partial(lambda hd, ci, c0: vl_ref[0, ci, vrows(hd), c0:c0 + _MLA_SUB], hd, ci, c0))
                          for hd in range(MLA_HEADS)]
        return units

    context_units = [(hd, functools.partial(lambda hd: kc_ref[0, :, lanes(hd)], hd),
                      functools.partial(lambda hd: vc_ref[0, 0, vrows(hd), :], hd)) for hd in range(MLA_HEADS)]
    niter = nchunk // unroll

    def body(it, carry):
        run(latent_units(it))
        return carry

    lax.fori_loop(0, niter - 1, body, 0)
    run(latent_units(niter - 1) + context_units)
    for pair in range(MLA_HEADS // 2):
        outs = [acc_ref[hd, :MLA_V] / acc_ref[hd, MLA_V:MLA_V + 1] for hd in (2 * pair, 2 * pair + 1)]
        o_ref[0, :, lanes(pair)] = jnp.transpose(jnp.concatenate(outs, axis=0)).astype(_BF)


def _mla_attention(q, k_lat, vt_lat, k_ctx, vt_ctx):
    b, t, _ = q.shape
    lc = k_ctx.shape[1]
    nchunk = vt_lat.shape[1]
    tq = 256
    return pl.pallas_call(
        functools.partial(_mla_kernel, nchunk=nchunk, tq=tq),
        grid=(b, t // tq),
        in_specs=[pl.BlockSpec((1, tq, 1024), lambda b, i: (b, i, 0)),
                  pl.BlockSpec((1, t, 1024), lambda b, i: (b, 0, 0)),
                  pl.BlockSpec((1,) + vt_lat.shape[1:], lambda b, i: (b, 0, 0, 0)),
                  pl.BlockSpec((1, lc, 1024), lambda b, i: (b, 0, 0)),
                  pl.BlockSpec((1,) + vt_ctx.shape[1:], lambda b, i: (b, 0, 0, 0))],
        out_specs=pl.BlockSpec((1, tq, 512), lambda b, i: (b, i, 0)),
        out_shape=jax.ShapeDtypeStruct((b, t, 512), _BF),
        scratch_shapes=[pltpu.VMEM((MLA_HEADS, 1, tq), _F32),
                        pltpu.VMEM((MLA_HEADS, _MLA_VT, tq), _F32)],
        compiler_params=_params("parallel", "parallel"),
        name="mla_attention",
    )(q, k_lat, vt_lat, k_ctx, vt_ctx)


def _rope_partner_cols(w, dim):
    q = dim // 4
    return jnp.concatenate([-w[:, q:2 * q], w[:, :q], -w[:, 3 * q:], w[:, 2 * q:3 * q]], axis=1)


def _rope_tables(t, dim):
    rows = t // GRID_W
    row = jnp.repeat(jnp.arange(rows), GRID_W).astype(_F32)
    col = jnp.tile(jnp.arange(GRID_W), rows).astype(_F32)
    nf = dim // 4
    inv = ROPE_THETA ** (-jnp.arange(nf, dtype=_F32) / nf)
    ar = row[:, None] * inv
    ac = col[:, None] * inv
    ang = jnp.concatenate([ar, ar, ac, ac], axis=-1)
    return jnp.cos(ang), jnp.sin(ang)


def _ab_weights(w_in, w_out):
    hd = WIN_HEAD_DIM
    g = WIN_HEADS // WIN_KV_HEADS
    glu = w_in[:, :1024]
    wq = w_in[:, 1024:1536]
    wk = w_in[:, 1536:1664]
    wv = w_in[:, 1664:1792]
    zeros = jnp.zeros_like(wk[:, :hd])
    qh = [wq[:, h * hd:(h + 1) * hd] for h in range(WIN_HEADS)]
    kh = [wk[:, h * hd:(h + 1) * hd] for h in range(WIN_KV_HEADS)]
    part = lambda w: _rope_partner_cols(w, hd)
    q_full = jnp.concatenate([jnp.concatenate([qh[j], qh[g + j]], 1) for j in range(g)], 1)
    q_part = jnp.concatenate([jnp.concatenate([part(qh[j]), part(qh[g + j])], 1) for j in range(g)], 1)
    k_full = jnp.concatenate([kh[0], zeros, zeros, kh[1]], 1)
    k_part = jnp.concatenate([part(kh[0]), zeros, zeros, part(kh[1])], 1)
    w_ext = jnp.concatenate([glu, q_full, k_full, wv, q_part, k_part], 1).astype(_BF)
    wa = w_out[:CONV_A_DIM].astype(_BF)
    wb_rows = w_out[CONV_A_DIM:]
    wb = jnp.concatenate([jnp.concatenate([wb_rows[j * hd:(j + 1) * hd], wb_rows[(g + j) * hd:(g + j + 1) * hd]], 0)
                          for j in range(g)], 0).astype(_BF)
    return w_ext, wa, wb


def _cd_weights(w_in, w_uq, w_ukv, lat):
    d = w_in.shape[0]
    kr = w_in[:, 1664:1696]
    z32 = jnp.zeros((d, MLA_ROPE), w_in.dtype)
    z64 = jnp.zeros((d, MLA_NOPE), w_in.dtype)
    if lat:
        kr128 = jnp.concatenate([z64, kr, z32], 1)
        krp128 = jnp.concatenate([z64, _rope_partner_cols(kr, MLA_ROPE), z32], 1)
    else:
        kr128 = jnp.concatenate([z64, z32, kr], 1)
        krp128 = jnp.zeros((d, LANES), w_in.dtype)
    w1 = jnp.concatenate([w_in[:, :1664], kr128, krp128], 1).astype(_BF)
    qd = MLA_NOPE + MLA_ROPE
    zq64 = jnp.zeros((MLA_Q_RANK, MLA_NOPE), w_uq.dtype)
    zq32 = jnp.zeros((MLA_Q_RANK, MLA_ROPE), w_uq.dtype)
    full, part = [], []
    for h in range(MLA_HEADS):
        nope = w_uq[:, h * qd:h * qd + MLA_NOPE]
        rope = w_uq[:, h * qd + MLA_NOPE:(h + 1) * qd]
        full += [nope, rope, rope]
        part += [zq64, _rope_partner_cols(rope, MLA_ROPE), zq32]
    wq = jnp.concatenate(full + part, 1).astype(_BF)
    kd = MLA_NOPE + MLA_V
    zk = jnp.zeros((MLA_KV_RANK, LANES - MLA_NOPE), w_ukv.dtype)
    kcols, vcols = [], []
    for h in range(MLA_HEADS):
        kcols += [w_ukv[:, h * kd:h * kd + MLA_NOPE], zk]
        vcols += [w_ukv[:, h * kd + MLA_NOPE:(h + 1) * kd]]
    wkv = jnp.concatenate(kcols + vcols, 1).astype(_BF)
    return w1, wq, wkv


def _ffn_weights(w_up, conv_w, conv_b, w_down):
    d = w_up.shape[0]
    wup = w_up.reshape(d, 2 * FFN_NCHUNK, FFN_CHUNK).transpose(1, 0, 2).astype(_BF)
    cw = conv_w.reshape(3, FFN_NCHUNK, FFN_CHUNK).transpose(1, 0, 2)
    cb = conv_b.reshape(FFN_NCHUNK, 1, FFN_CHUNK)
    wdn = w_down.reshape(FFN_NCHUNK, FFN_CHUNK, d).astype(_BF)
    return wup, cw, cb, wdn


def _block_diag_dense(w):
    nb, hh, kk = w.shape
    eye = jnp.eye(nb, dtype=w.dtype)
    return (eye[:, None, :, None] * w[:, :, None, :]).reshape(nb * hh, nb * kk)


def kernel(x, c, ctx, c_ctx, w_mod, b_mod, norm_g, ffn_w_up, ffn_conv_w, ffn_conv_b, ffn_w_down, ab_w_in, a_conv_w, a_conv_b, a_ln_g, a_ln_b, b_sink, ab_w_out, cd_w_in, lru_conv_w, lru_conv_b, lru_gate_w, lru_gate_b, lru_lambda, mla_q_norm, mla_w_uq, mla_kv_norm, mla_w_ukv, cd_w_out):
    b, t, d = x.shape
    lc = ctx.shape[1]
    ctx_row = 8 * ((b + 7) // 8)
    cc = jnp.zeros((ctx_row + 8, d), _F32).at[:b].set(c).at[ctx_row].set(c_ctx)
    mods = _modulation(cc, w_mod, b_mod).reshape(w_mod.shape[0], ctx_row + 8, N_MOD, d)

    mod, g = mods[0], norm_g[0]
    w_ext, wa, wb = _ab_weights(ab_w_in[0], ab_w_out[0])
    cos64, sin64 = _rope_tables(t, WIN_HEAD_DIM)
    cos_l = jnp.concatenate([cos64, cos64], 1)
    sin_l = jnp.concatenate([sin64, sin64], 1)
    cos_c = jnp.ones((lc, LANES), _F32)
    sin_c = jnp.zeros((lc, LANES), _F32)
    glu_l, q_l, k_l, v_l = _ab_in(x, mod, None, g, w_ext, cos_l * _WIN_QSCALE, sin_l * _WIN_QSCALE, cos_l, sin_l)
    glu_c, q_c, k_c, v_c = _ab_in(ctx, mod, ctx_row, g, w_ext, cos_c * _WIN_QSCALE, sin_c, cos_c, sin_c)
    a_l = _conformer_conv(glu_l, a_conv_w[0], a_conv_b[0], a_ln_g[0], a_ln_b[0])
    a_c = _conformer_conv(glu_c, a_conv_w[0], a_conv_b[0], a_ln_g[0], a_ln_b[0])
    b_l = _win_attention(q_l, k_l, v_l, k_c, v_c, b_sink[0], True)
    b_c = _win_attention(q_c, k_c, v_c, k_c, v_c, b_sink[0], False)
    xl, hl = _mixer_out(a_l, b_l, wa, wb, x, mod, None, g)
    xc, hc = _mixer_out(a_c, b_c, wa, wb, ctx, mod, ctx_row, g)
    fw = _ffn_weights(ffn_w_up[0], ffn_conv_w[0], ffn_conv_b[0], ffn_w_down[0])
    xl = _ffn(hl, xl, *fw, mod, None, g)
    xc = _ffn(hc, xc, *fw, mod, ctx_row, g)

    mod, g = mods[1], norm_g[1]
    w1_l, wq, wkv = _cd_weights(cd_w_in[0], mla_w_uq[0], mla_w_ukv[0], True)
    w1_c, _, _ = _cd_weights(cd_w_in[0], mla_w_uq[0], mla_w_ukv[0], False)
    cos32, sin32 = _rope_tables(t, MLA_ROPE)
    one64, zero64 = jnp.ones((t, MLA_NOPE), _F32), jnp.zeros((t, MLA_NOPE), _F32)
    one32, zero32 = jnp.ones((t, MLA_ROPE), _F32), jnp.zeros((t, MLA_ROPE), _F32)
    cos_t = jnp.concatenate([one64, cos32, one32], 1)
    sin_t = jnp.concatenate([zero64, sin32, zero32], 1)
    qn = mla_q_norm[0].reshape(1, -1)
    kvn = mla_kv_norm[0].reshape(1, -1)
    qs = (MLA_NOPE + MLA_ROPE) ** -0.5 * _LOG2E
    xg_l, q_l, k_l, v_l = _cd_in(xl, mod, None, g, w1_l, qn, wq, kvn, wkv, cos_t * qs, sin_t * qs, cos_t, sin_t)
    xg_c, _, k_c, v_c = _cd_in(xc, mod, ctx_row, g, w1_c, qn, wq, kvn, wkv, cos_c, sin_c, cos_c, sin_c)
    wg = [jnp.concatenate([_block_diag_dense(lru_gate_w[0, dr, 0]), _block_diag_dense(lru_gate_w[0, dr, 1])], 1).astype(_BF)
          for dr in range(2)]
    gb = [jnp.concatenate([lru_gate_b[0, dr, 0], lru_gate_b[0, dr, 1]]) for dr in range(2)]
    lru = lambda xg, h0, dr, hb=None: _rglru(xg, h0, lru_conv_w[0, dr], lru_conv_b[0, dr], wg[dr], gb[dr],
                                             lru_lambda[0, dr], dr == 1, hb)
    h_zero = jnp.zeros((b, LRU_DIM), _F32)
    hf_c = lru(xg_c, h_zero, 0)
    hb_c = lru(xg_c, h_zero, 1)
    hb_l = lru(xg_l, hb_c[:, 0], 1)
    c_l = lru(xg_l, hf_c[:, lc - 1], 0, hb_l)
    d_l = _mla_attention(q_l, k_l, v_l, k_c, v_c)
    wo = cd_w_out[0]
    xl, hl = _mixer_out(c_l, d_l, wo[:LRU_DIM].astype(_BF), wo[LRU_DIM:].astype(_BF), xl, mod, None, g)
    fw = _ffn_weights(ffn_w_up[1], ffn_conv_w[1], ffn_conv_b[1], ffn_w_down[1])
    return _ffn(hl, xl, *fw, mod, None, g)
```

```python
import functools

import jax
import jax.numpy as jnp
import numpy as np
from jax import lax
from jax.experimental import pallas as pl
from jax.experimental.pallas import tpu as pltpu

_BF = jnp.bfloat16
_F32 = jnp.float32

D_MODEL = 1024
N_MOD = 6
EPS = 1e-6
ROPE_THETA = 10000.0
GRID_W = 64
NEG_INF = -1e30

CONV_A_DIM = 512
CONV_A_WIDTH = 31
WIN_HEADS = 8
WIN_KV_HEADS = 2
WIN_HEAD_DIM = 64
WINDOW = 128
LRU_DIM = 512
LRU_BLOCKS = 8
LRU_CONV_WIDTH = 4
LRU_C = 8.0
MLA_HEADS = 8
MLA_Q_RANK = 384
MLA_KV_RANK = 256
MLA_NOPE = 64
MLA_ROPE = 32
MLA_V = 64
FFN_DIM = 2816
FFN_CHUNK = 256
FFN_NCHUNK = FFN_DIM // FFN_CHUNK
LANES = 128
_SLAB = 8
HALO = 16
VMEM_LIMIT = 56 * 1024 * 1024
_LOG2E = 1.4426950408889634
_MLA_TK = 512
_MLA_VT = 80
_MLA_SUB = 512
_MLA_AHEAD = 3
_WIN_VT = 80
_WIN_QSCALE = float(np.float32(WIN_HEAD_DIM ** -0.5 * _LOG2E))
_WIN_AHEAD = 3
_MLA_UNROLL = 4


def _params(*sem):
    return pltpu.CompilerParams(dimension_semantics=sem, vmem_limit_bytes=VMEM_LIMIT)


def _const_spec(shape):
    nd = len(shape)
    return pl.BlockSpec(shape, lambda *_: (0,) * nd, pipeline_mode=pl.Buffered(1))


def _rms(x, g):
    return x * lax.rsqrt(jnp.mean(x * x, axis=-1, keepdims=True) + EPS) * g


def _dot(a, b):
    return jnp.dot(a, b, preferred_element_type=_F32)


def _dot_nt(a, b):
    return lax.dot_general(a, b, (((1,), (1,)), ((), ())), preferred_element_type=_F32)


def _rope(x, cos, sin_up, sin_dn, quarter):
    return x * cos + pltpu.roll(x, LANES - quarter, 1) * sin_up + pltpu.roll(x, quarter, 1) * sin_dn


def _mod_spec(mod_row):
    if mod_row is None:
        return pl.BlockSpec((1, N_MOD, D_MODEL), lambda b, i: (b, 0, 0))
    return pl.BlockSpec((1, N_MOD, D_MODEL), lambda b, i: (mod_row, 0, 0))


def _mod_kernel(c_ref, w_ref, b_ref, o_ref):
    c = c_ref[...]
    s = (c * jax.nn.sigmoid(c)).astype(_BF)
    o_ref[0] = _dot(s, w_ref[0].astype(_BF)) + b_ref[0]


def _modulation(cc, w_mod, b_mod):
    depth, d, n = w_mod.shape
    tn = 512
    return pl.pallas_call(
        _mod_kernel,
        grid=(depth, n // tn),
        in_specs=[pl.BlockSpec(cc.shape, lambda l, j: (0, 0)),
                  pl.BlockSpec((1, d, tn), lambda l, j: (l, 0, j)),
                  pl.BlockSpec((1, 1, tn), lambda l, j: (l, 0, j))],
        out_specs=pl.BlockSpec((1, cc.shape[0], tn), lambda l, j: (l, 0, j)),
        out_shape=jax.ShapeDtypeStruct((depth, cc.shape[0], n), _F32),
        compiler_params=_params("parallel", "parallel"),
        name="modulation",
    )(cc, w_mod, b_mod.reshape(depth, 1, n))


_AB_Q0, _AB_K0, _AB_V0, _AB_N = 1024, 1536, 1792, 1920


def _ab_in_kernel(x_ref, mod_ref, g_ref, w_ref, qt_ref, kt_ref, glu_ref, q_ref, k_ref, vt_ref):
    half = x_ref.shape[1] // 2
    rows = [slice(r * half, (r + 1) * half) for r in range(2)]
    zs = [_dot((_rms(x_ref[0, r, :], g_ref[0:1]) * (1.0 + mod_ref[0, 1:2]) + mod_ref[0, 0:1]).astype(_BF), w_ref[...])
          for r in rows]
    quarter = WIN_HEAD_DIM // 4
    ones = jnp.ones((_WIN_VT - WIN_HEAD_DIM, LANES), _BF)
    for hi, (r, z) in enumerate(zip(rows, zs)):
        glu_ref[0, r, :] = z[:, :_AB_Q0]
        qt = [qt_ref[i, r, :] for i in range(3)]
        kt = [kt_ref[i, r, :] for i in range(3)]
        for j in range(4):
            qf = z[:, _AB_Q0 + LANES * j:_AB_Q0 + LANES * (j + 1)]
            q_ref[0, j, r, :] = _rope(qf, *qt, quarter).astype(_BF)
            q_ref[0, 4 + j, r, :] = (qf * _WIN_QSCALE).astype(_BF)
        for j in range(2):
            kf = z[:, _AB_K0 + LANES * j:_AB_K0 + LANES * (j + 1)]
            k_ref[0, r, LANES * j:LANES * (j + 1)] = _rope(kf, *kt, quarter).astype(_BF)
        vt = jnp.transpose(z[:, _AB_V0:_AB_N]).astype(_BF)
        nblk = half // LANES
        for cb in range(nblk):
            for kv in range(WIN_KV_HEADS):
                vt_ref[0, hi * nblk + cb, kv * _WIN_VT:kv * _WIN_VT + WIN_HEAD_DIM, :] = (
                    vt[kv * WIN_HEAD_DIM:(kv + 1) * WIN_HEAD_DIM, cb * LANES:(cb + 1) * LANES])
                vt_ref[0, hi * nblk + cb, kv * _WIN_VT + WIN_HEAD_DIM:(kv + 1) * _WIN_VT, :] = ones


def _ab_in(x, mod, mod_row, g, w_ext, qtab, ktab):
    b, t, d = x.shape
    tm = min(512, t)
    tab = pl.BlockSpec((3, tm, LANES), lambda b, i: (0, i, 0))
    return pl.pallas_call(
        _ab_in_kernel,
        grid=(b, t // tm),
        in_specs=[pl.BlockSpec((1, tm, d), lambda b, i: (b, i, 0)),
                  _mod_spec(mod_row),
                  _const_spec(g.shape),
                  _const_spec(w_ext.shape),
                  tab, tab],
        out_specs=[pl.BlockSpec((1, tm, 1024), lambda b, i: (b, i, 0)),
                   pl.BlockSpec((1, 8, tm, LANES), lambda b, i: (b, 0, i, 0)),
                   pl.BlockSpec((1, tm, 256), lambda b, i: (b, i, 0)),
                   pl.BlockSpec((1, tm // LANES, WIN_KV_HEADS * _WIN_VT, LANES), lambda b, i: (b, i, 0, 0))],
        out_shape=[jax.ShapeDtypeStruct((b, t, 1024), _F32),
                   jax.ShapeDtypeStruct((b, 8, t, LANES), _BF),
                   jax.ShapeDtypeStruct((b, t, 256), _BF),
                   jax.ShapeDtypeStruct((b, t // LANES, WIN_KV_HEADS * _WIN_VT, LANES), _BF)],
        compiler_params=_params("parallel", "parallel"),
        name="ab_in",
    )(x, mod, g, w_ext, qtab, ktab)


_CONV_ROWS = 64


def _conv_kernel(glu_ref, prev_ref, next_ref, w_ref, cb_ref, lg_ref, lb_ref, o_ref, uext_ref, *, tt, nt):
    i = pl.program_id(1)

    def glu(z):
        return z[:, :CONV_A_DIM] * jax.nn.sigmoid(z[:, CONV_A_DIM:])

    uext_ref[0, 0:HALO] = jnp.where(i > 0, glu(prev_ref[0]), 0.0)
    uext_ref[0, HALO:HALO + tt] = glu(glu_ref[0])
    uext_ref[0, HALO + tt:2 * HALO + tt] = jnp.where(i < nt - 1, glu(next_ref[0]), 0.0)
    span = tt + 2 * HALO - _SLAB
    for p in range(1, _SLAB):
        uext_ref[p, 0:span] = uext_ref[0, p:p + span]
    half = CONV_A_WIDTH // 2
    for r in range(tt // _CONV_ROWS):
        acc = jnp.zeros((_CONV_ROWS, CONV_A_DIM), _F32)
        for j in range(CONV_A_WIDTH):
            off = r * _CONV_ROWS + HALO - half + j
            p = off % _SLAB
            acc = acc + w_ref[j:j + 1, :] * uext_ref[p, off - p:off - p + _CONV_ROWS, :]
        u = acc + cb_ref[...]
        mu = jnp.mean(u, axis=-1, keepdims=True)
        var = jnp.mean(jnp.square(u - mu), axis=-1, keepdims=True)
        y = (u - mu) * lax.rsqrt(var + EPS) * lg_ref[...] + lb_ref[...]
        o_ref[0, r * _CONV_ROWS:(r + 1) * _CONV_ROWS, :] = (y * jax.nn.sigmoid(y)).astype(_BF)


def _conformer_conv(glu, conv_w, conv_b, ln_g, ln_b):
    b, t, _ = glu.shape
    tt = min(512, t)
    nt = t // tt
    hb = tt // HALO
    return pl.pallas_call(
        functools.partial(_conv_kernel, tt=tt, nt=nt),
        grid=(b, nt),
        in_specs=[pl.BlockSpec((1, tt, 1024), lambda b, i: (b, i, 0)),
                  pl.BlockSpec((1, HALO, 1024), lambda b, i: (b, jnp.maximum(i * hb - 1, 0), 0)),
                  pl.BlockSpec((1, HALO, 1024), lambda b, i: (b, jnp.minimum((i + 1) * hb, t // HALO - 1), 0)),
                  _const_spec(conv_w.shape),
                  _const_spec((1, CONV_A_DIM)), _const_spec((1, CONV_A_DIM)), _const_spec((1, CONV_A_DIM))],
        out_specs=pl.BlockSpec((1, tt, CONV_A_DIM), lambda b, i: (b, i, 0)),
        out_shape=jax.ShapeDtypeStruct((b, t, CONV_A_DIM), _BF),
        scratch_shapes=[pltpu.VMEM((_SLAB, tt + 2 * HALO, CONV_A_DIM), _F32)],
        compiler_params=_params("parallel", "parallel"),
        name="conformer_conv",
    )(glu, glu, glu, conv_w, conv_b.reshape(1, -1), ln_g.reshape(1, -1), ln_b.reshape(1, -1))


def _win_kernel(sink_ref, q_ref, kl_ref, vl_ref, kc_ref, vc_ref, o_ref, *, t, tq, has_lat):
    n = pl.program_id(1)
    g = WIN_HEADS // WIN_KV_HEADS
    nctx = kc_ref.shape[1] // LANES
    if has_lat:
        band = tq + 2 * WINDOW
        start = pl.multiple_of(jnp.clip(n * tq - WINDOW, 0, t - band), LANES)
        kpos = start + lax.broadcasted_iota(jnp.int32, (band, 1), 0)
        qpos = n * tq + lax.broadcasted_iota(jnp.int32, (1, tq), 1)
        bias = jnp.where(jnp.abs(qpos - kpos) <= WINDOW, 0.0, NEG_INF)

    def kcols(kv):
        return slice(kv * LANES, (kv + 1) * LANES)

    def vrows(kv):
        return slice(kv * _WIN_VT, (kv + 1) * _WIN_VT)

    def scores(kv, j):
        s_ctx = _dot_nt(kc_ref[0, :, kcols(kv)], q_ref[0, (g + j) if has_lat else j])
        if not has_lat:
            return (s_ctx,)
        return s_ctx, _dot_nt(kl_ref[0, pl.ds(start, band), kcols(kv)], q_ref[0, j]) + bias

    def finish(kv, j, s):
        t_sink = jnp.full((1, tq), sink_ref[kv * g + j] * _LOG2E, _F32)
        m = t_sink
        for part in s:
            m = jnp.maximum(m, jnp.max(part, axis=0, keepdims=True))
        vt_ctx = jnp.concatenate([vc_ref[0, cb, vrows(kv), :] for cb in range(nctx)], axis=1)
        acc = _dot(vt_ctx, jnp.exp2(s[0] - m).astype(_BF))
        if has_lat:
            sb = start // LANES
            vt_lat = jnp.concatenate([vl_ref[0, sb + cb, vrows(kv), :] for cb in range(band // LANES)], axis=1)
            acc = acc + _dot(vt_lat, jnp.exp2(s[1] - m).astype(_BF))
        den = acc[WIN_HEAD_DIM:WIN_HEAD_DIM + 1] + jnp.exp2(t_sink - m)
        return acc[:WIN_HEAD_DIM] / den

    units = [(kv, j) for j in range(g) for kv in range(WIN_KV_HEADS)]
    s = {u: scores(*units[u]) for u in range(min(_WIN_AHEAD, len(units)))}
    outs = {}
    for u, (kv, j) in enumerate(units):
        nxt = u + _WIN_AHEAD
        if nxt < len(units):
            s[nxt] = scores(*units[nxt])
        outs[(kv, j)] = finish(kv, j, s.pop(u))
        if kv == WIN_KV_HEADS - 1:
            both = jnp.concatenate([outs.pop((k2, j)) for k2 in range(WIN_KV_HEADS)], axis=0)
            o_ref[0, :, j * LANES:(j + 1) * LANES] = jnp.transpose(both).astype(_BF)


def _win_attention(q, k_lat, vt_lat, k_ctx, vt_ctx, sink, has_lat):
    b, _, t, _ = q.shape
    lc = k_ctx.shape[1]
    tq = 256 if has_lat else t
    tl = k_lat.shape[1]
    return pl.pallas_call(
        functools.partial(_win_kernel, t=t, tq=tq, has_lat=has_lat),
        grid=(b, t // tq),
        in_specs=[pl.BlockSpec(memory_space=pltpu.SMEM),
                  pl.BlockSpec((1, 8, tq, LANES), lambda b, n: (b, 0, n, 0)),
                  pl.BlockSpec((1, tl, 256), lambda b, n: (b, 0, 0)),
                  pl.BlockSpec((1,) + vt_lat.shape[1:], lambda b, n: (b, 0, 0, 0)),
                  pl.BlockSpec((1, lc, 256), lambda b, n: (b, 0, 0)),
                  pl.BlockSpec((1,) + vt_ctx.shape[1:], lambda b, n: (b, 0, 0, 0))],
        out_specs=pl.BlockSpec((1, tq, 512), lambda b, n: (b, n, 0)),
        out_shape=jax.ShapeDtypeStruct((b, t, 512), _BF),
        compiler_params=_params("parallel", "parallel"),
        name="win_attention" if has_lat else "ctx_attention",
    )(sink, q, k_lat, vt_lat, k_ctx, vt_ctx)


def _out_kernel(a_ref, b_ref, wa_ref, wb_ref, x_ref, mod_ref, g_ref, xo_ref, h_ref):
    half = a_ref.shape[1] // 2
    rows = [slice(r * half, (r + 1) * half) for r in range(2)]
    ys = [_dot(a_ref[0, r, :], wa_ref[...]) + _dot(b_ref[0, r, :], wb_ref[...]) for r in rows]
    for r, y in zip(rows, ys):
        x1 = x_ref[0, r, :] + mod_ref[0, 2:3] * _rms(y, g_ref[1:2])
        xo_ref[0, r, :] = x1
        h_ref[0, r, :] = (_rms(x1, g_ref[2:3]) * (1.0 + mod_ref[0, 4:5]) + mod_ref[0, 3:4]).astype(_BF)


def _mixer_out(a, bb, wa, wb, x, mod, mod_row, g):
    b, t, d = x.shape
    tm = min(512, t)
    return pl.pallas_call(
        _out_kernel,
        grid=(b, t // tm),
        in_specs=[pl.BlockSpec((1, tm, 512), lambda b, i: (b, i, 0)),
                  pl.BlockSpec((1, tm, 512), lambda b, i: (b, i, 0)),
                  _const_spec(wa.shape), _const_spec(wb.shape),
                  pl.BlockSpec((1, tm, d), lambda b, i: (b, i, 0)),
                  _mod_spec(mod_row),
                  _const_spec(g.shape)],
        out_specs=[pl.BlockSpec((1, tm, d), lambda b, i: (b, i, 0)),
                   pl.BlockSpec((1, tm, d), lambda b, i: (b, i, 0))],
        out_shape=[jax.ShapeDtypeStruct((b, t, d), _F32),
                   jax.ShapeDtypeStruct((b, t, d), _BF)],
        compiler_params=_params("parallel", "parallel"),
        name="mixer_out",
    )(a, bb, wa, wb, x, mod, g)


def _ffn_kernel(h_ref, hp_ref, hn_ref, x_ref, wup_ref, cw_ref, cb_ref, wdn_ref, mod_ref, g_ref,
                o_ref, gbuf_ref, acc_ref, *, tm, nt):
    i = pl.program_id(1)
    cur = h_ref[0]
    lhs = jnp.concatenate([hp_ref[0], cur, hn_ref[0]], axis=0)
    row = lax.broadcasted_iota(jnp.int32, (tm + 2 * HALO, 1), 0)
    keep = jnp.logical_and(jnp.logical_or(i > 0, row >= HALO), jnp.logical_or(i < nt - 1, row < HALO + tm))

    def cols(c):
        return slice(c * FFN_CHUNK, (c + 1) * FFN_CHUNK)

    def up(c):
        return (jnp.where(keep, _dot(lhs, wup_ref[:, cols(c)]), 0.0),
                _dot(cur, wup_ref[:, cols(FFN_NCHUNK + c)]))

    def down(c, g, u):
        buf = gbuf_ref.at[c % 2]
        buf[...] = g
        w = cw_ref[:, cols(c)]
        gc = (w[0:1] * buf[HALO - 1:HALO - 1 + tm, :] + w[1:2] * buf[HALO:HALO + tm, :]
              + w[2:3] * buf[HALO + 1:HALO + 1 + tm, :] + cb_ref[:, cols(c)])
        y = _dot((jax.nn.gelu(gc) * u).astype(_BF), wdn_ref[cols(c), :])
        if c == 0:
            acc_ref[...] = y
        else:
            acc_ref[...] += y

    pending = up(0)
    for c in range(FFN_NCHUNK):
        following = up(c + 1) if c + 1 < FFN_NCHUNK else None
        down(c, *pending)
        pending = following
    o_ref[0] = x_ref[0] + mod_ref[0, 5:6] * _rms(acc_ref[...], g_ref[3:4])


def _ffn(h, x, wup, cw, cb, wdn, mod, mod_row, g):
    b, t, d = x.shape
    tm = min(512, t)
    nt = t // tm
    hb = tm // HALO
    return pl.pallas_call(
        functools.partial(_ffn_kernel, tm=tm, nt=nt),
        grid=(b, nt),
        in_specs=[pl.BlockSpec((1, tm, d), lambda b, i: (b, i, 0)),
                  pl.BlockSpec((1, HALO, d), lambda b, i: (b, jnp.maximum(i * hb - 1, 0), 0)),
                  pl.BlockSpec((1, HALO, d), lambda b, i: (b, jnp.minimum((i + 1) * hb, t // HALO - 1), 0)),
                  pl.BlockSpec((1, tm, d), lambda b, i: (b, i, 0)),
                  _const_spec(wup.shape), _const_spec(cw.shape), _const_spec(cb.shape), _const_spec(wdn.shape),
                  _mod_spec(mod_row),
                  _const_spec(g.shape)],
        out_specs=pl.BlockSpec((1, tm, d), lambda b, i: (b, i, 0)),
        out_shape=jax.ShapeDtypeStruct((b, t, d), _F32),
        scratch_shapes=[pltpu.VMEM((2, tm + 2 * HALO, FFN_CHUNK), _F32),
                        pltpu.VMEM((tm, d), _F32)],
        compiler_params=_params("parallel", "parallel"),
        name="conv_glu_ffn",
    )(h, h, h, x, wup, cw, cb, wdn, mod, g)


_CD_CQ0, _CD_CKV0, _CD_KR0, _CD_N = 1024, 1408, 1664, 1792


def _cd_in_kernel(x_ref, mod_ref, g_ref, w1_ref, qn_ref, wq_ref, kvn_ref, wkv_ref,
                  qt_ref, kt_ref, xg_ref, q_ref, k_ref, v_ref):
    half = x_ref.shape[1] // 2
    rows = [slice(r * half, (r + 1) * half) for r in range(2)]
    zs = [_dot((_rms(x_ref[0, r, :], g_ref[0:1]) * (1.0 + mod_ref[0, 1:2]) + mod_ref[0, 0:1]).astype(_BF), w1_ref[...])
          for r in rows]
    ups = [(_dot(_rms(z[:, _CD_CQ0:_CD_CKV0], qn_ref[...]).astype(_BF), wq_ref[...]),
            _dot(_rms(z[:, _CD_CKV0:_CD_KR0], kvn_ref[...]).astype(_BF), wkv_ref[...])) for z in zs]
    quarter = MLA_ROPE // 4
    nq = MLA_HEADS * LANES
    ones = jnp.ones((_MLA_VT - MLA_V, half), _BF)
    for r, z, (qq, kv) in zip(rows, zs, ups):
        xg_ref[0, r, :] = z[:, :_CD_CQ0]
        qt = [qt_ref[i, r, :] for i in range(3)]
        for hd in range(MLA_HEADS):
            lo, hi = hd * LANES, (hd + 1) * LANES
            q_ref[0, r, lo:hi] = _rope(qq[:, lo:hi], *qt, quarter).astype(_BF)
        krr = _rope(z[:, _CD_KR0:_CD_N], *[kt_ref[i, r, :] for i in range(3)], quarter)
        for hd in range(MLA_HEADS):
            lo, hi = hd * LANES, (hd + 1) * LANES
            k_ref[0, r, lo:hi] = (kv[:, lo:hi] + krr).astype(_BF)
        vt = jnp.transpose(kv[:, nq:]).astype(_BF)
        for hd in range(MLA_HEADS):
            v_ref[0, 0, hd * _MLA_VT:hd * _MLA_VT + MLA_V, r] = vt[hd * MLA_V:(hd + 1) * MLA_V]
            v_ref[0, 0, hd * _MLA_VT + MLA_V:(hd + 1) * _MLA_VT, r] = ones


def _cd_in(x, mod, mod_row, g, w1, qn, wq, kvn, wkv, qtab, ktab):
    b, t, d = x.shape
    tm = min(512, t)
    tab = pl.BlockSpec((3, tm, LANES), lambda b, i: (0, i, 0))
    return pl.pallas_call(
        _cd_in_kernel,
        grid=(b, t // tm),
        in_specs=[pl.BlockSpec((1, tm, d), lambda b, i: (b, i, 0)),
                  _mod_spec(mod_row),
                  _const_spec(g.shape), _const_spec(w1.shape), _const_spec(qn.shape), _const_spec(wq.shape),
                  _const_spec(kvn.shape), _const_spec(wkv.shape), tab, tab],
        out_specs=[pl.BlockSpec((1, tm, 1024), lambda b, i: (b, i, 0)),
                   pl.BlockSpec((1, tm, 1024), lambda b, i: (b, i, 0)),
                   pl.BlockSpec((1, tm, 1024), lambda b, i: (b, i, 0)),
                   pl.BlockSpec((1, 1, MLA_HEADS * _MLA_VT, tm), lambda b, i: (b, i, 0, 0))],
        out_shape=[jax.ShapeDtypeStruct((b, t, 1024), _F32),
                   jax.ShapeDtypeStruct((b, t, 1024), _BF),
                   jax.ShapeDtypeStruct((b, t, 1024), _BF),
                   jax.ShapeDtypeStruct((b, t // tm, MLA_HEADS * _MLA_VT, tm), _BF)],
        compiler_params=_params("parallel", "parallel"),
        name="cd_in",
    )(x, mod, g, w1, qn, wq, kvn, wkv, qtab, ktab)


def _lru_kernel(*refs, tt, reverse, combine):
    if combine:
        (x_ref, h0_ref, cw_ref, cb_ref, wg_ref, gb_ref, lam_ref, hb_ref, gt_ref,
         o_ref, xext_ref, a_ref, b_ref, hcar_ref, xcar_ref, hs_ref) = refs
    else:
        (x_ref, h0_ref, cw_ref, cb_ref, wg_ref, gb_ref, lam_ref,
         o_ref, xext_ref, a_ref, b_ref, hcar_ref, xcar_ref) = refs
        hs_ref = o_ref.at[0]
    i = pl.program_id(1)

    @pl.when(i == 0)
    def _():
        hcar_ref[...] = jnp.broadcast_to(h0_ref[0], (_SLAB, LRU_DIM))
        xcar_ref[...] = jnp.zeros_like(xcar_ref)

    x = x_ref[0]
    if reverse:
        xext_ref[0:tt] = x
        xext_ref[tt:tt + _SLAB] = xcar_ref[...]
        xcar_ref[...] = x[0:_SLAB]
        base = 0
    else:
        xext_ref[0:_SLAB] = xcar_ref[...]
        xext_ref[_SLAB:_SLAB + tt] = x
        xcar_ref[...] = x[tt - _SLAB:tt]
        base = _SLAB - (LRU_CONV_WIDTH - 1)
    xc = cb_ref[...]
    for j in range(LRU_CONV_WIDTH):
        xc = xc + cw_ref[j:j + 1, :] * xext_ref[base + j:base + j + tt, :]
    gates = _dot(xc.astype(_BF), wg_ref[...]) + gb_ref[...]
    r = jax.nn.sigmoid(gates[:, :LRU_DIM])
    ig = jax.nn.sigmoid(gates[:, LRU_DIM:])
    nl = -lam_ref[...]
    neg_sp = jnp.maximum(nl, 0.0) + jnp.log1p(jnp.exp(-jnp.abs(nl)))
    log_a = -LRU_C * r * neg_sp
    th = jnp.tanh(log_a)
    a_ref[...] = jnp.exp(log_a)
    y = -2.0 * th / (1.0 - th)
    b_ref[...] = jnp.where(y > 0.0, y * lax.rsqrt(y), 0.0) * ig * xc

    ns = tt // _SLAB
    rowi = lax.broadcasted_iota(jnp.int32, (_SLAB, LRU_DIM), 0)

    def slab(s, h):
        r0 = pl.multiple_of(((ns - 1 - s) if reverse else s) * _SLAB, _SLAB)
        av = a_ref[pl.ds(r0, _SLAB), :]
        bv = b_ref[pl.ds(r0, _SLAB), :]
        for sh in (1, 2, 4):
            if reverse:
                ok = rowi < _SLAB - sh
                a_sh = pltpu.roll(av, _SLAB - sh, 0)
                b_sh = pltpu.roll(bv, _SLAB - sh, 0)
            else:
                ok = rowi >= sh
                a_sh = pltpu.roll(av, sh, 0)
                b_sh = pltpu.roll(bv, sh, 0)
            bv = jnp.where(ok, av * b_sh + bv, bv)
            av = jnp.where(ok, av * a_sh, av)
        hs = av * h + bv
        hs_ref[pl.ds(r0, _SLAB), :] = hs
        last = hs[0:1] if reverse else hs[_SLAB - 1:_SLAB]
        return jnp.broadcast_to(last, (_SLAB, LRU_DIM))

    hcar_ref[...] = lax.fori_loop(0, ns, slab, hcar_ref[...])
    if combine:
        o_ref[0] = ((hs_ref[...] + hb_ref[0]) * jax.nn.gelu(gt_ref[0])).astype(o_ref.dtype)


def _rglru(xg, h0, conv_w, conv_b, wg, gate_b, lam, reverse, hb=None):
    b, t, _ = xg.shape
    tt = min(256, t)
    nt = t // tt
    combine = hb is not None
    tidx = (lambda i: nt - 1 - i) if reverse else (lambda i: i)
    in_specs = [pl.BlockSpec((1, tt, LRU_DIM), lambda b, i: (b, tidx(i), 0)),
                pl.BlockSpec((1, 1, LRU_DIM), lambda b, i: (b, 0, 0)),
                _const_spec(conv_w.shape), _const_spec((1, LRU_DIM)), _const_spec(wg.shape),
                _const_spec((1, 2 * LRU_DIM)), _const_spec((1, LRU_DIM))]
    args = [xg, h0.reshape(b, 1, LRU_DIM), conv_w, conv_b.reshape(1, -1), wg, gate_b.reshape(1, -1), lam.reshape(1, -1)]
    if combine:
        in_specs += [pl.BlockSpec((1, tt, LRU_DIM), lambda b, i: (b, tidx(i), 0)),
                     pl.BlockSpec((1, tt, LRU_DIM), lambda b, i: (b, tidx(i), 1))]
        args += [hb, xg]
    return pl.pallas_call(
        functools.partial(_lru_kernel, tt=tt, reverse=reverse, combine=combine),
        grid=(b, nt),
        in_specs=in_specs,
        out_specs=pl.BlockSpec((1, tt, LRU_DIM), lambda b, i: (b, tidx(i), 0)),
        out_shape=jax.ShapeDtypeStruct((b, t, LRU_DIM), _BF if combine else _F32),
        scratch_shapes=[pltpu.VMEM((tt + _SLAB, LRU_DIM), _F32),
                        pltpu.VMEM((tt, LRU_DIM), _F32),
                        pltpu.VMEM((tt, LRU_DIM), _F32),
                        pltpu.VMEM((_SLAB, LRU_DIM), _F32),
                        pltpu.VMEM((_SLAB, LRU_DIM), _F32)]
        + ([pltpu.VMEM((tt, LRU_DIM), _F32)] if combine else []),
        compiler_params=_params("parallel", "arbitrary"),
        name="rglru_combine" if combine else ("rglru_rev" if reverse else "rglru_fwd"),
    )(*args)


def _mla_kernel(q_ref, kl_ref, vl_ref, kc_ref, vc_ref, o_ref, m_ref, acc_ref, *, nchunk, tq):
    m_ref[...] = jnp.full(m_ref.shape, NEG_INF, _F32)
    acc_ref[...] = jnp.zeros(acc_ref.shape, _F32)

    def lanes(hd):
        return slice(hd * LANES, (hd + 1) * LANES)

    def vrows(hd):
        return slice(hd * _MLA_VT, (hd + 1) * _MLA_VT)

    def scores(hd, k):
        return _dot_nt(k, q_ref[0, :, lanes(hd)])

    def update(hd, s, vt):
        m = m_ref[hd]
        m_new = jnp.maximum(m, jnp.max(s, axis=0, keepdims=True))
        p = jnp.exp2(s - m_new).astype(_BF)
        acc_ref[hd] = acc_ref[hd] * jnp.exp2(m - m_new) + _dot(vt, p)
        m_ref[hd] = m_new

    def run(units):
        s = {j: scores(units[j][0], units[j][1]()) for j in range(min(_MLA_AHEAD, len(units)))}
        for j, (hd, _, values) in enumerate(units):
            nxt = j + _MLA_AHEAD
            if nxt < len(units):
                s[nxt] = scores(units[nxt][0], units[nxt][1]())
            update(hd, s.pop(j), values())

    unroll = max(u for u in range(1, _MLA_UNROLL + 1) if nchunk % u == 0)

    def latent_units(it):
        units = []
        for sub in range(unroll):
            ci = it * unroll + sub
            r0 = ci * _MLA_TK if isinstance(ci, int) else pl.multiple_of(ci * _MLA_TK, _MLA_TK)
            for c0 in range(0, _MLA_TK, _MLA_SUB):
                units += [(hd, functools.partial(lambda hd, r: kl_ref[0, pl.ds(r, _MLA_SUB), lanes(hd)], hd, r0 + c0),
                           functools.partial(lambda hd, ci, c0: vl_ref[0, ci, vrows(hd), c0:c0 + _MLA_SUB], hd, ci, c0))
                          for hd in range(MLA_HEADS)]
        return units

    context_units = [(hd, functools.partial(lambda hd: kc_ref[0, :, lanes(hd)], hd),
                      functools.partial(lambda hd: vc_ref[0, 0, vrows(hd), :], hd)) for hd in range(MLA_HEADS)]
    niter = nchunk // unroll

    def body(it, carry):
        run(latent_units(it))
        return carry

    lax.fori_loop(0, niter - 1, body, 0)
    run(latent_units(niter - 1) + context_units)
    for pair in range(MLA_HEADS // 2):
        outs = [acc_ref[hd, :MLA_V] / acc_ref[hd, MLA_V:MLA_V + 1] for hd in (2 * pair, 2 * pair + 1)]
        o_ref[0, :, lanes(pair)] = jnp.transpose(jnp.concatenate(outs, axis=0)).astype(_BF)


def _mla_attention(q, k_lat, vt_lat, k_ctx, vt_ctx):
    b, t, _ = q.shape
    lc = k_ctx.shape[1]
    nchunk = vt_lat.shape[1]
    tq = 256
    return pl.pallas_call(
        functools.partial(_mla_kernel, nchunk=nchunk, tq=tq),
        grid=(b, t // tq),
        in_specs=[pl.BlockSpec((1, tq, 1024), lambda b, i: (b, i, 0)),
                  pl.BlockSpec((1, t, 1024), lambda b, i: (b, 0, 0)),
                  pl.BlockSpec((1,) + vt_lat.shape[1:], lambda b, i: (b, 0, 0, 0)),
                  pl.BlockSpec((1, lc, 1024), lambda b, i: (b, 0, 0)),
                  pl.BlockSpec((1,) + vt_ctx.shape[1:], lambda b, i: (b, 0, 0, 0))],
        out_specs=pl.BlockSpec((1, tq, 512), lambda b, i: (b, i, 0)),
        out_shape=jax.ShapeDtypeStruct((b, t, 512), _BF),
        scratch_shapes=[pltpu.VMEM((MLA_HEADS, 1, tq), _F32),
                        pltpu.VMEM((MLA_HEADS, _MLA_VT, tq), _F32)],
        compiler_params=_params("parallel", "parallel"),
        name="mla_attention",
    )(q, k_lat, vt_lat, k_ctx, vt_ctx)


def _rope_tables(t, dim):
    rows = t // GRID_W
    row = np.repeat(np.arange(rows), GRID_W).astype(np.float32)
    col = np.tile(np.arange(GRID_W), rows).astype(np.float32)
    nf = dim // 4
    inv = (np.float32(ROPE_THETA) ** (-np.arange(nf, dtype=np.float32) / np.float32(nf))).astype(np.float32)
    ar = row[:, None] * inv
    ac = col[:, None] * inv
    ang = np.concatenate([ar, ar, ac, ac], axis=-1).astype(np.float32)
    return np.cos(ang).astype(np.float32), np.sin(ang).astype(np.float32)


def _rope_slot_tables(cos, sin, lo, scale=1.0):
    t, dim = cos.shape
    q = dim // 4
    sign = np.concatenate([-np.ones(q), np.zeros(q), -np.ones(q), np.zeros(q)]).astype(np.float32)
    out = np.zeros((3, t, LANES), np.float32)
    out[0] = 1.0
    for l0 in (lo if isinstance(lo, (tuple, list)) else (lo,)):
        out[0, :, l0:l0 + dim] = cos
        out[1, :, l0:l0 + dim] = sin * sign
        out[2, :, l0:l0 + dim] = sin * (1.0 + sign)
    return out * np.float32(scale)


def _ab_weights(w_in, w_out):
    hd = WIN_HEAD_DIM
    g = WIN_HEADS // WIN_KV_HEADS
    glu = w_in[:, :1024]
    wq = w_in[:, 1024:1536]
    wk = w_in[:, 1536:1664]
    wv = w_in[:, 1664:1792]
    zeros = jnp.zeros_like(wk[:, :hd])
    qh = [wq[:, h * hd:(h + 1) * hd] for h in range(WIN_HEADS)]
    kh = [wk[:, h * hd:(h + 1) * hd] for h in range(WIN_KV_HEADS)]
    q_full = jnp.concatenate([jnp.concatenate([qh[j], qh[g + j]], 1) for j in range(g)], 1)
    k_full = jnp.concatenate([kh[0], zeros, zeros, kh[1]], 1)
    w_ext = jnp.concatenate([glu, q_full, k_full, wv], 1).astype(_BF)
    wa = w_out[:CONV_A_DIM].astype(_BF)
    wb_rows = w_out[CONV_A_DIM:]
    wb = jnp.concatenate([jnp.concatenate([wb_rows[j * hd:(j + 1) * hd], wb_rows[(g + j) * hd:(g + j + 1) * hd]], 0)
                          for j in range(g)], 0).astype(_BF)
    return w_ext, wa, wb


def _cd_weights(w_in, w_uq, w_ukv, lat):
    d = w_in.shape[0]
    kr = w_in[:, 1664:1696]
    z32 = jnp.zeros((d, MLA_ROPE), w_in.dtype)
    z64 = jnp.zeros((d, MLA_NOPE), w_in.dtype)
    kr128 = jnp.concatenate([z64, kr, z32] if lat else [z64, z32, kr], 1)
    w1 = jnp.concatenate([w_in[:, :1664], kr128], 1).astype(_BF)
    qd = MLA_NOPE + MLA_ROPE
    full = []
    for h in range(MLA_HEADS):
        nope = w_uq[:, h * qd:h * qd + MLA_NOPE]
        rope = w_uq[:, h * qd + MLA_NOPE:(h + 1) * qd]
        full += [nope, rope, rope]
    wq = jnp.concatenate(full, 1).astype(_BF)
    kd = MLA_NOPE + MLA_V
    zk = jnp.zeros((MLA_KV_RANK, LANES - MLA_NOPE), w_ukv.dtype)
    kcols, vcols = [], []
    for h in range(MLA_HEADS):
        kcols += [w_ukv[:, h * kd:h * kd + MLA_NOPE], zk]
        vcols += [w_ukv[:, h * kd + MLA_NOPE:(h + 1) * kd]]
    wkv = jnp.concatenate(kcols + vcols, 1).astype(_BF)
    return w1, wq, wkv


def _ffn_weights(w_up, conv_w, conv_b, w_down):
    return w_up.astype(_BF), conv_w, conv_b.reshape(1, -1), w_down.astype(_BF)


def _block_diag_dense(w):
    nb, hh, kk = w.shape
    eye = jnp.eye(nb, dtype=w.dtype)
    return (eye[:, None, :, None] * w[:, :, None, :]).reshape(nb * hh, nb * kk)


def kernel(x, c, ctx, c_ctx, w_mod, b_mod, norm_g, ffn_w_up, ffn_conv_w, ffn_conv_b, ffn_w_down, ab_w_in, a_conv_w, a_conv_b, a_ln_g, a_ln_b, b_sink, ab_w_out, cd_w_in, lru_conv_w, lru_conv_b, lru_gate_w, lru_gate_b, lru_lambda, mla_q_norm, mla_w_uq, mla_kv_norm, mla_w_ukv, cd_w_out):
    b, t, d = x.shape
    lc = ctx.shape[1]
    ctx_row = 8 * ((b + 7) // 8)
    cc = jnp.zeros((ctx_row + 8, d), _F32).at[:b].set(c).at[ctx_row].set(c_ctx)
    mods = _modulation(cc, w_mod, b_mod).reshape(w_mod.shape[0], ctx_row + 8, N_MOD, d)

    mod, g = mods[0], norm_g[0]
    w_ext, wa, wb = _ab_weights(ab_w_in[0], ab_w_out[0])
    cos64, sin64 = _rope_tables(t, WIN_HEAD_DIM)
    halves = (0, WIN_HEAD_DIM)
    ident = _rope_slot_tables(np.ones((lc, 4), np.float32), np.zeros((lc, 4), np.float32), 0)
    glu_l, q_l, k_l, v_l = _ab_in(x, mod, None, g, w_ext, _rope_slot_tables(cos64, sin64, halves, _WIN_QSCALE),
                                  _rope_slot_tables(cos64, sin64, halves))
    glu_c, q_c, k_c, v_c = _ab_in(ctx, mod, ctx_row, g, w_ext, ident * np.float32(_WIN_QSCALE), ident)
    a_l = _conformer_conv(glu_l, a_conv_w[0], a_conv_b[0], a_ln_g[0], a_ln_b[0])
    a_c = _conformer_conv(glu_c, a_conv_w[0], a_conv_b[0], a_ln_g[0], a_ln_b[0])
    b_l = _win_attention(q_l, k_l, v_l, k_c, v_c, b_sink[0], True)
    b_c = _win_attention(q_c, k_c, v_c, k_c, v_c, b_sink[0], False)
    xl, hl = _mixer_out(a_l, b_l, wa, wb, x, mod, None, g)
    xc, hc = _mixer_out(a_c, b_c, wa, wb, ctx, mod, ctx_row, g)
    fw = _ffn_weights(ffn_w_up[0], ffn_conv_w[0], ffn_conv_b[0], ffn_w_down[0])
    xl = _ffn(hl, xl, *fw, mod, None, g)
    xc = _ffn(hc, xc, *fw, mod, ctx_row, g)

    mod, g = mods[1], norm_g[1]
    w1_l, wq, wkv = _cd_weights(cd_w_in[0], mla_w_uq[0], mla_w_ukv[0], True)
    w1_c, _, _ = _cd_weights(cd_w_in[0], mla_w_uq[0], mla_w_ukv[0], False)
    cos32, sin32 = _rope_tables(t, MLA_ROPE)
    qn = mla_q_norm[0].reshape(1, -1)
    kvn = mla_kv_norm[0].reshape(1, -1)
    qs = np.float32((MLA_NOPE + MLA_ROPE) ** -0.5 * _LOG2E)
    xg_l, q_l, k_l, v_l = _cd_in(xl, mod, None, g, w1_l, qn, wq, kvn, wkv, _rope_slot_tables(cos32, sin32, MLA_NOPE, qs),
                                 _rope_slot_tables(cos32, sin32, MLA_NOPE))
    xg_c, _, k_c, v_c = _cd_in(xc, mod, ctx_row, g, w1_c, qn, wq, kvn, wkv, ident, ident)
    wg = [jnp.concatenate([_block_diag_dense(lru_gate_w[0, dr, 0]), _block_diag_dense(lru_gate_w[0, dr, 1])], 1).astype(_BF)
          for dr in range(2)]
    gb = [jnp.concatenate([lru_gate_b[0, dr, 0], lru_gate_b[0, dr, 1]]) for dr in range(2)]
    lru = lambda xg, h0, dr, hb=None: _rglru(xg, h0, lru_conv_w[0, dr], lru_conv_b[0, dr], wg[dr], gb[dr],
                                             lru_lambda[0, dr], dr == 1, hb)
    h_zero = jnp.zeros((b, LRU_DIM), _F32)
    hf_c = lru(xg_c, h_zero, 0)
    hb_c = lru(xg_c, h_zero, 1)
    hb_l = lru(xg_l, hb_c[:, 0], 1)
    c_l = lru(xg_l, hf_c[:, lc - 1], 0, hb_l)
    d_l = _mla_attention(q_l, k_l, v_l, k_c, v_c)
    wo = cd_w_out[0]
    xl, hl = _mixer_out(c_l, d_l, wo[:LRU_DIM].astype(_BF), wo[LRU_DIM:].astype(_BF), xl, mod, None, g)
    fw = _ffn_weights(ffn_w_up[1], ffn_conv_w[1], ffn_conv_b[1], ffn_w_down[1])
    return _ffn(hl, xl, *fw, mod, None, g)
```

```python
import functools

import jax
import jax.numpy as jnp
import numpy as np
from jax import lax
from jax.experimental import pallas as pl
from jax.experimental.pallas import tpu as pltpu

_BF = jnp.bfloat16
_F32 = jnp.float32

D_MODEL = 1024
N_MOD = 6
EPS = 1e-6
ROPE_THETA = 10000.0
GRID_W = 64
NEG_INF = -1e30

CONV_A_DIM = 512
CONV_A_WIDTH = 31
WIN_HEADS = 8
WIN_KV_HEADS = 2
WIN_HEAD_DIM = 64
WINDOW = 128
LRU_DIM = 512
LRU_BLOCKS = 8
LRU_CONV_WIDTH = 4
LRU_C = 8.0
MLA_HEADS = 8
MLA_Q_RANK = 384
MLA_KV_RANK = 256
MLA_NOPE = 64
MLA_ROPE = 32
MLA_V = 64
FFN_DIM = 2816
FFN_CHUNK = 256
FFN_NCHUNK = FFN_DIM // FFN_CHUNK
_FFN_DOWN = 1
LANES = 128
_SLAB = 8
HALO = 16
VMEM_LIMIT = 56 * 1024 * 1024
_LOG2E = 1.4426950408889634
_MLA_TK = 512
_MLA_VT = 80
_MLA_SUB = 512
_MLA_AHEAD = 3
_WIN_VT = 80
_WIN_QSCALE = float(np.float32(WIN_HEAD_DIM ** -0.5 * _LOG2E))
_WIN_AHEAD = 4
_MLA_UNROLL = 8


def _params(*sem):
    return pltpu.CompilerParams(dimension_semantics=sem, vmem_limit_bytes=VMEM_LIMIT)


def _const_spec(shape):
    nd = len(shape)
    return pl.BlockSpec(shape, lambda *_: (0,) * nd, pipeline_mode=pl.Buffered(1))


def _rms(x, g):
    return x * lax.rsqrt(jnp.mean(x * x, axis=-1, keepdims=True) + EPS) * g


def _dot(a, b):
    return jnp.dot(a, b, preferred_element_type=_F32)


def _dot_nt(a, b):
    return lax.dot_general(a, b, (((1,), (1,)), ((), ())), preferred_element_type=_F32)


def _rope(x, cos, sin_up, sin_dn, quarter):
    return x * cos + pltpu.roll(x, LANES - quarter, 1) * sin_up + pltpu.roll(x, quarter, 1) * sin_dn


def _mod_spec(mod_row):
    if mod_row is None:
        return pl.BlockSpec((1, N_MOD, D_MODEL), lambda b, i: (b, 0, 0))
    return pl.BlockSpec((1, N_MOD, D_MODEL), lambda b, i: (mod_row, 0, 0))


def _mod_kernel(c_ref, w_ref, b_ref, o_ref):
    c = c_ref[...]
    s = (c * jax.nn.sigmoid(c)).astype(_BF)
    o_ref[0] = _dot(s, w_ref[0].astype(_BF)) + b_ref[0]


def _modulation(cc, w_mod, b_mod):
    depth, d, n = w_mod.shape
    tn = 512
    return pl.pallas_call(
        _mod_kernel,
        grid=(depth, n // tn),
        in_specs=[pl.BlockSpec(cc.shape, lambda l, j: (0, 0)),
                  pl.BlockSpec((1, d, tn), lambda l, j: (l, 0, j)),
                  pl.BlockSpec((1, 1, tn), lambda l, j: (l, 0, j))],
        out_specs=pl.BlockSpec((1, cc.shape[0], tn), lambda l, j: (l, 0, j)),
        out_shape=jax.ShapeDtypeStruct((depth, cc.shape[0], n), _F32),
        compiler_params=_params("parallel", "parallel"),
        name="modulation",
    )(cc, w_mod, b_mod.reshape(depth, 1, n))


_AB_Q0, _AB_K0, _AB_V0, _AB_N = 1024, 1536, 1792, 1920


def _ab_in_kernel(x_ref, mod_ref, g_ref, w_ref, qt_ref, kt_ref, glu_ref, q_ref, k_ref, vt_ref):
    half = x_ref.shape[1] // 2
    rows = [slice(r * half, (r + 1) * half) for r in range(2)]
    zs = [_dot((_rms(x_ref[0, r, :], g_ref[0:1]) * (1.0 + mod_ref[0, 1:2]) + mod_ref[0, 0:1]).astype(_BF), w_ref[...])
          for r in rows]
    quarter = WIN_HEAD_DIM // 4
    ones = jnp.ones((_WIN_VT - WIN_HEAD_DIM, LANES), _BF)
    for hi, (r, z) in enumerate(zip(rows, zs)):
        glu_ref[0, r, :] = z[:, :_AB_Q0]
        qt = [qt_ref[i, r, :] for i in range(3)]
        kt = [kt_ref[i, r, :] for i in range(3)]
        for j in range(4):
            qf = z[:, _AB_Q0 + LANES * j:_AB_Q0 + LANES * (j + 1)]
            q_ref[0, j, r, :] = _rope(qf, *qt, quarter).astype(_BF)
            q_ref[0, 4 + j, r, :] = (qf * _WIN_QSCALE).astype(_BF)
        for j in range(2):
            kf = z[:, _AB_K0 + LANES * j:_AB_K0 + LANES * (j + 1)]
            k_ref[0, r, LANES * j:LANES * (j + 1)] = _rope(kf, *kt, quarter).astype(_BF)
        vt = jnp.transpose(z[:, _AB_V0:_AB_N]).astype(_BF)
        nblk = half // LANES
        for cb in range(nblk):
            for kv in range(WIN_KV_HEADS):
                vt_ref[0, hi * nblk + cb, kv * _WIN_VT:kv * _WIN_VT + WIN_HEAD_DIM, :] = (
                    vt[kv * WIN_HEAD_DIM:(kv + 1) * WIN_HEAD_DIM, cb * LANES:(cb + 1) * LANES])
                vt_ref[0, hi * nblk + cb, kv * _WIN_VT + WIN_HEAD_DIM:(kv + 1) * _WIN_VT, :] = ones


def _ab_in(x, mod, mod_row, g, w_ext, qtab, ktab):
    b, t, d = x.shape
    tm = min(512, t)
    tab = pl.BlockSpec((3, tm, LANES), lambda b, i: (0, i, 0))
    return pl.pallas_call(
        _ab_in_kernel,
        grid=(b, t // tm),
        in_specs=[pl.BlockSpec((1, tm, d), lambda b, i: (b, i, 0)),
                  _mod_spec(mod_row),
                  _const_spec(g.shape),
                  _const_spec(w_ext.shape),
                  tab, tab],
        out_specs=[pl.BlockSpec((1, tm, 1024), lambda b, i: (b, i, 0)),
                   pl.BlockSpec((1, 8, tm, LANES), lambda b, i: (b, 0, i, 0)),
                   pl.BlockSpec((1, tm, 256), lambda b, i: (b, i, 0)),
                   pl.BlockSpec((1, tm // LANES, WIN_KV_HEADS * _WIN_VT, LANES), lambda b, i: (b, i, 0, 0))],
        out_shape=[jax.ShapeDtypeStruct((b, t, 1024), _F32),
                   jax.ShapeDtypeStruct((b, 8, t, LANES), _BF),
                   jax.ShapeDtypeStruct((b, t, 256), _BF),
                   jax.ShapeDtypeStruct((b, t // LANES, WIN_KV_HEADS * _WIN_VT, LANES), _BF)],
        compiler_params=_params("parallel", "parallel"),
        name="ab_in",
    )(x, mod, g, w_ext, qtab, ktab)


_CONV_ROWS = 64


def _conv_kernel(glu_ref, prev_ref, next_ref, w_ref, cb_ref, lg_ref, lb_ref, o_ref, uext_ref, *, tt, nt):
    i = pl.program_id(1)

    def glu(z):
        return z[:, :CONV_A_DIM] * jax.nn.sigmoid(z[:, CONV_A_DIM:])

    uext_ref[0, 0:HALO] = jnp.where(i > 0, glu(prev_ref[0]), 0.0)
    uext_ref[0, HALO:HALO + tt] = glu(glu_ref[0])
    uext_ref[0, HALO + tt:2 * HALO + tt] = jnp.where(i < nt - 1, glu(next_ref[0]), 0.0)
    span = tt + 2 * HALO - _SLAB
    for p in range(1, _SLAB):
        uext_ref[p, 0:span] = uext_ref[0, p:p + span]
    half = CONV_A_WIDTH // 2
    for r in range(tt // _CONV_ROWS):
        acc = jnp.zeros((_CONV_ROWS, CONV_A_DIM), _F32)
        for j in range(CONV_A_WIDTH):
            off = r * _CONV_ROWS + HALO - half + j
            p = off % _SLAB
            acc = acc + w_ref[j:j + 1, :] * uext_ref[p, off - p:off - p + _CONV_ROWS, :]
        u = acc + cb_ref[...]
        mu = jnp.mean(u, axis=-1, keepdims=True)
        var = jnp.mean(jnp.square(u - mu), axis=-1, keepdims=True)
        y = (u - mu) * lax.rsqrt(var + EPS) * lg_ref[...] + lb_ref[...]
        o_ref[0, r * _CONV_ROWS:(r + 1) * _CONV_ROWS, :] = (y * jax.nn.sigmoid(y)).astype(_BF)


def _conformer_conv(glu, conv_w, conv_b, ln_g, ln_b):
    b, t, _ = glu.shape
    tt = min(512, t)
    nt = t // tt
    hb = tt // HALO
    return pl.pallas_call(
        functools.partial(_conv_kernel, tt=tt, nt=nt),
        grid=(b, nt),
        in_specs=[pl.BlockSpec((1, tt, 1024), lambda b, i: (b, i, 0)),
                  pl.BlockSpec((1, HALO, 1024), lambda b, i: (b, jnp.maximum(i * hb - 1, 0), 0)),
                  pl.BlockSpec((1, HALO, 1024), lambda b, i: (b, jnp.minimum((i + 1) * hb, t // HALO - 1), 0)),
                  _const_spec(conv_w.shape),
                  _const_spec((1, CONV_A_DIM)), _const_spec((1, CONV_A_DIM)), _const_spec((1, CONV_A_DIM))],
        out_specs=pl.BlockSpec((1, tt, CONV_A_DIM), lambda b, i: (b, i, 0)),
        out_shape=jax.ShapeDtypeStruct((b, t, CONV_A_DIM), _BF),
        scratch_shapes=[pltpu.VMEM((_SLAB, tt + 2 * HALO, CONV_A_DIM), _F32)],
        compiler_params=_params("parallel", "parallel"),
        name="conformer_conv",
    )(glu, glu, glu, conv_w, conv_b.reshape(1, -1), ln_g.reshape(1, -1), ln_b.reshape(1, -1))


def _win_kernel(sink_ref, q_ref, kl_ref, vl_ref, kc_ref, vc_ref, o_ref, *, t, tq, has_lat):
    n = pl.program_id(1)
    g = WIN_HEADS // WIN_KV_HEADS
    nctx = kc_ref.shape[1] // LANES
    if has_lat:
        band = tq + 2 * WINDOW
        start = pl.multiple_of(jnp.clip(n * tq - WINDOW, 0, t - band), LANES)
        kpos = start + lax.broadcasted_iota(jnp.int32, (band, 1), 0)
        qpos = n * tq + lax.broadcasted_iota(jnp.int32, (1, tq), 1)
        bias = jnp.where(jnp.abs(qpos - kpos) <= WINDOW, 0.0, NEG_INF)

    def kcols(kv):
        return slice(kv * LANES, (kv + 1) * LANES)

    def vrows(kv):
        return slice(kv * _WIN_VT, (kv + 1) * _WIN_VT)

    def scores(kv, j):
        s_ctx = _dot_nt(kc_ref[0, :, kcols(kv)], q_ref[0, (g + j) if has_lat else j])
        if not has_lat:
            return (s_ctx,)
        return s_ctx, _dot_nt(kl_ref[0, pl.ds(start, band), kcols(kv)], q_ref[0, j]) + bias

    def finish(kv, j, s):
        t_sink = jnp.full((1, tq), sink_ref[kv * g + j] * _LOG2E, _F32)
        m = t_sink
        for part in s:
            m = jnp.maximum(m, jnp.max(part, axis=0, keepdims=True))
        vt_ctx = jnp.concatenate([vc_ref[0, cb, vrows(kv), :] for cb in range(nctx)], axis=1)
        acc = _dot(vt_ctx, jnp.exp2(s[0] - m).astype(_BF))
        if has_lat:
            sb = start // LANES
            vt_lat = jnp.concatenate([vl_ref[0, sb + cb, vrows(kv), :] for cb in range(band // LANES)], axis=1)
            acc = acc + _dot(vt_lat, jnp.exp2(s[1] - m).astype(_BF))
        den = acc[WIN_HEAD_DIM:WIN_HEAD_DIM + 1] + jnp.exp2(t_sink - m)
        return acc[:WIN_HEAD_DIM] / den

    units = [(kv, j) for j in range(g) for kv in range(WIN_KV_HEADS)]
    s = {u: scores(*units[u]) for u in range(min(_WIN_AHEAD, len(units)))}
    outs = {}
    for u, (kv, j) in enumerate(units):
        nxt = u + _WIN_AHEAD
        if nxt < len(units):
            s[nxt] = scores(*units[nxt])
        outs[(kv, j)] = finish(kv, j, s.pop(u))
        if kv == WIN_KV_HEADS - 1:
            both = jnp.concatenate([outs.pop((k2, j)) for k2 in range(WIN_KV_HEADS)], axis=0)
            o_ref[0, :, j * LANES:(j + 1) * LANES] = jnp.transpose(both).astype(_BF)


def _win_attention(q, k_lat, vt_lat, k_ctx, vt_ctx, sink, has_lat):
    b, _, t, _ = q.shape
    lc = k_ctx.shape[1]
    tq = 256 if has_lat else t
    tl = k_lat.shape[1]
    return pl.pallas_call(
        functools.partial(_win_kernel, t=t, tq=tq, has_lat=has_lat),
        grid=(b, t // tq),
        in_specs=[pl.BlockSpec(memory_space=pltpu.SMEM),
                  pl.BlockSpec((1, 8, tq, LANES), lambda b, n: (b, 0, n, 0)),
                  pl.BlockSpec((1, tl, 256), lambda b, n: (b, 0, 0)),
                  pl.BlockSpec((1,) + vt_lat.shape[1:], lambda b, n: (b, 0, 0, 0)),
                  pl.BlockSpec((1, lc, 256), lambda b, n: (b, 0, 0)),
                  pl.BlockSpec((1,) + vt_ctx.shape[1:], lambda b, n: (b, 0, 0, 0))],
        out_specs=pl.BlockSpec((1, tq, 512), lambda b, n: (b, n, 0)),
        out_shape=jax.ShapeDtypeStruct((b, t, 512), _BF),
        compiler_params=_params("parallel", "parallel"),
        name="win_attention" if has_lat else "ctx_attention",
    )(sink, q, k_lat, vt_lat, k_ctx, vt_ctx)


def _out_kernel(a_ref, b_ref, wa_ref, wb_ref, x_ref, mod_ref, g_ref, xo_ref, h_ref):
    half = a_ref.shape[1] // 2
    rows = [slice(r * half, (r + 1) * half) for r in range(2)]
    ys = [_dot(a_ref[0, r, :], wa_ref[...]) + _dot(b_ref[0, r, :], wb_ref[...]) for r in rows]
    for r, y in zip(rows, ys):
        x1 = x_ref[0, r, :] + mod_ref[0, 2:3] * _rms(y, g_ref[1:2])
        xo_ref[0, r, :] = x1
        h_ref[0, r, :] = (_rms(x1, g_ref[2:3]) * (1.0 + mod_ref[0, 4:5]) + mod_ref[0, 3:4]).astype(_BF)


def _mixer_out(a, bb, wa, wb, x, mod, mod_row, g):
    b, t, d = x.shape
    tm = min(512, t)
    return pl.pallas_call(
        _out_kernel,
        grid=(b, t // tm),
        in_specs=[pl.BlockSpec((1, tm, 512), lambda b, i: (b, i, 0)),
                  pl.BlockSpec((1, tm, 512), lambda b, i: (b, i, 0)),
                  _const_spec(wa.shape), _const_spec(wb.shape),
                  pl.BlockSpec((1, tm, d), lambda b, i: (b, i, 0)),
                  _mod_spec(mod_row),
                  _const_spec(g.shape)],
        out_specs=[pl.BlockSpec((1, tm, d), lambda b, i: (b, i, 0)),
                   pl.BlockSpec((1, tm, d), lambda b, i: (b, i, 0))],
        out_shape=[jax.ShapeDtypeStruct((b, t, d), _F32),
                   jax.ShapeDtypeStruct((b, t, d), _BF)],
        compiler_params=_params("parallel", "parallel"),
        name="mixer_out",
    )(a, bb, wa, wb, x, mod, g)


def _ffn_kernel(h_ref, hp_ref, hn_ref, x_ref, wup_ref, cw_ref, cb_ref, wdn_ref, mod_ref, g_ref,
                o_ref, gbuf_ref, acc_ref, *, tm, nt):
    i = pl.program_id(1)
    cur = h_ref[0]
    lhs = jnp.concatenate([hp_ref[0], cur, hn_ref[0]], axis=0)
    edges = ((HALO - 1, i > 0), (HALO + tm, i < nt - 1))

    def cols(c):
        return slice(c * FFN_CHUNK, (c + 1) * FFN_CHUNK)

    def gate(c):
        return _dot(lhs, wup_ref[:, cols(c)])

    def activation(c, g):
        buf = gbuf_ref.at[c % 2]
        buf[...] = g
        for r, inside in edges:
            buf[r:r + 1, :] = jnp.where(inside, buf[r:r + 1, :], 0.0)
        w = cw_ref[:, cols(c)]
        gc = (w[0:1] * buf[HALO - 1:HALO - 1 + tm, :] + w[1:2] * buf[HALO:HALO + tm, :]
              + w[2:3] * buf[HALO + 1:HALO + 1 + tm, :] + cb_ref[:, cols(c)])
        u = _dot(cur, wup_ref[:, cols(FFN_NCHUNK + c)])
        return (jax.nn.gelu(gc) * u).astype(_BF)

    pending = gate(0)
    acts = []
    for c in range(FFN_NCHUNK):
        following = gate(c + 1) if c + 1 < FFN_NCHUNK else None
        acts.append(activation(c, pending))
        pending = following
        if len(acts) == _FFN_DOWN or c == FFN_NCHUNK - 1:
            c0 = c + 1 - len(acts)
            y = _dot(jnp.concatenate(acts, axis=1), wdn_ref[c0 * FFN_CHUNK:(c + 1) * FFN_CHUNK, :])
            if c0 == 0:
                acc_ref[...] = y
            else:
                acc_ref[...] += y
            acts = []
    o_ref[0] = x_ref[0] + mod_ref[0, 5:6] * _rms(acc_ref[...], g_ref[3:4])


def _ffn(h, x, wup, cw, cb, wdn, mod, mod_row, g):
    b, t, d = x.shape
    tm = min(512, t)
    nt = t // tm
    hb = tm // HALO
    return pl.pallas_call(
        functools.partial(_ffn_kernel, tm=tm, nt=nt),
        grid=(b, nt),
        in_specs=[pl.BlockSpec((1, tm, d), lambda b, i: (b, i, 0)),
                  pl.BlockSpec((1, HALO, d), lambda b, i: (b, jnp.maximum(i * hb - 1, 0), 0)),
                  pl.BlockSpec((1, HALO, d), lambda b, i: (b, jnp.minimum((i + 1) * hb, t // HALO - 1), 0)),
                  pl.BlockSpec((1, tm, d), lambda b, i: (b, i, 0)),
                  _const_spec(wup.shape), _const_spec(cw.shape), _const_spec(cb.shape), _const_spec(wdn.shape),
                  _mod_spec(mod_row),
                  _const_spec(g.shape)],
        out_specs=pl.BlockSpec((1, tm, d), lambda b, i: (b, i, 0)),
        out_shape=jax.ShapeDtypeStruct((b, t, d), _F32),
        scratch_shapes=[pltpu.VMEM((2, tm + 2 * HALO, FFN_CHUNK), _F32),
                        pltpu.VMEM((tm, d), _F32)],
        compiler_params=_params("parallel", "parallel"),
        name="conv_glu_ffn",
    )(h, h, h, x, wup, cw, cb, wdn, mod, g)


_CD_CQ0, _CD_CKV0, _CD_KR0, _CD_N = 1024, 1408, 1664, 1792


def _cd_in_kernel(x_ref, mod_ref, g_ref, w1_ref, qn_ref, wq_ref, kvn_ref, wkv_ref,
                  qt_ref, kt_ref, xg_ref, q_ref, k_ref, v_ref):
    half = x_ref.shape[1] // 2
    rows = [slice(r * half, (r + 1) * half) for r in range(2)]
    zs = [_dot((_rms(x_ref[0, r, :], g_ref[0:1]) * (1.0 + mod_ref[0, 1:2]) + mod_ref[0, 0:1]).astype(_BF), w1_ref[...])
          for r in rows]
    ups = [(_dot(_rms(z[:, _CD_CQ0:_CD_CKV0], qn_ref[...]).astype(_BF), wq_ref[...]),
            _dot(_rms(z[:, _CD_CKV0:_CD_KR0], kvn_ref[...]).astype(_BF), wkv_ref[...])) for z in zs]
    quarter = MLA_ROPE // 4
    nq = MLA_HEADS * LANES
    ones = jnp.ones((_MLA_VT - MLA_V, half), _BF)
    for r, z, (qq, kv) in zip(rows, zs, ups):
        xg_ref[0, r, :] = z[:, :_CD_CQ0]
        qt = [qt_ref[i, r, :] for i in range(3)]
        for hd in range(MLA_HEADS):
            lo, hi = hd * LANES, (hd + 1) * LANES
            q_ref[0, r, lo:hi] = _rope(qq[:, lo:hi], *qt, quarter).astype(_BF)
        krr = _rope(z[:, _CD_KR0:_CD_N], *[kt_ref[i, r, :] for i in range(3)], quarter)
        for hd in range(MLA_HEADS):
            lo, hi = hd * LANES, (hd + 1) * LANES
            k_ref[0, r, lo:hi] = (kv[:, lo:hi] + krr).astype(_BF)
        vt = jnp.transpose(kv[:, nq:]).astype(_BF)
        for hd in range(MLA_HEADS):
            v_ref[0, 0, hd * _MLA_VT:hd * _MLA_VT + MLA_V, r] = vt[hd * MLA_V:(hd + 1) * MLA_V]
            v_ref[0, 0, hd * _MLA_VT + MLA_V:(hd + 1) * _MLA_VT, r] = ones


def _cd_in(x, mod, mod_row, g, w1, qn, wq, kvn, wkv, qtab, ktab):
    b, t, d = x.shape
    tm = min(512, t)
    tab = pl.BlockSpec((3, tm, LANES), lambda b, i: (0, i, 0))
    return pl.pallas_call(
        _cd_in_kernel,
        grid=(b, t // tm),
        in_specs=[pl.BlockSpec((1, tm, d), lambda b, i: (b, i, 0)),
                  _mod_spec(mod_row),
                  _const_spec(g.shape), _const_spec(w1.shape), _const_spec(qn.shape), _const_spec(wq.shape),
                  _const_spec(kvn.shape), _const_spec(wkv.shape), tab, tab],
        out_specs=[pl.BlockSpec((1, tm, 1024), lambda b, i: (b, i, 0)),
                   pl.BlockSpec((1, tm, 1024), lambda b, i: (b, i, 0)),
                   pl.BlockSpec((1, tm, 1024), lambda b, i: (b, i, 0)),
                   pl.BlockSpec((1, 1, MLA_HEADS * _MLA_VT, tm), lambda b, i: (b, i, 0, 0))],
        out_shape=[jax.ShapeDtypeStruct((b, t, 1024), _F32),
                   jax.ShapeDtypeStruct((b, t, 1024), _BF),
                   jax.ShapeDtypeStruct((b, t, 1024), _BF),
                   jax.ShapeDtypeStruct((b, t // tm, MLA_HEADS * _MLA_VT, tm), _BF)],
        compiler_params=_params("parallel", "parallel"),
        name="cd_in",
    )(x, mod, g, w1, qn, wq, kvn, wkv, qtab, ktab)


def _lru_kernel(*refs, tt, reverse, combine):
    if combine:
        (x_ref, h0_ref, cw_ref, cb_ref, wg_ref, gb_ref, lam_ref, hb_ref, gt_ref,
         o_ref, xext_ref, a_ref, b_ref, hcar_ref, xcar_ref, hs_ref) = refs
    else:
        (x_ref, h0_ref, cw_ref, cb_ref, wg_ref, gb_ref, lam_ref,
         o_ref, xext_ref, a_ref, b_ref, hcar_ref, xcar_ref) = refs
        hs_ref = o_ref.at[0]
    i = pl.program_id(1)

    @pl.when(i == 0)
    def _():
        hcar_ref[...] = jnp.broadcast_to(h0_ref[0], (_SLAB, LRU_DIM))
        xcar_ref[...] = jnp.zeros_like(xcar_ref)

    x = x_ref[0]
    if reverse:
        xext_ref[0:tt] = x
        xext_ref[tt:tt + _SLAB] = xcar_ref[...]
        xcar_ref[...] = x[0:_SLAB]
        base = 0
    else:
        xext_ref[0:_SLAB] = xcar_ref[...]
        xext_ref[_SLAB:_SLAB + tt] = x
        xcar_ref[...] = x[tt - _SLAB:tt]
        base = _SLAB - (LRU_CONV_WIDTH - 1)
    xc = cb_ref[...]
    for j in range(LRU_CONV_WIDTH):
        xc = xc + cw_ref[j:j + 1, :] * xext_ref[base + j:base + j + tt, :]
    gates = _dot(xc.astype(_BF), wg_ref[...]) + gb_ref[...]
    r = jax.nn.sigmoid(gates[:, :LRU_DIM])
    ig = jax.nn.sigmoid(gates[:, LRU_DIM:])
    nl = -lam_ref[...]
    neg_sp = jnp.maximum(nl, 0.0) + jnp.log1p(jnp.exp(-jnp.abs(nl)))
    log_a = -LRU_C * r * neg_sp
    th = jnp.tanh(log_a)
    a_ref[...] = jnp.exp(log_a)
    y = -2.0 * th / (1.0 - th)
    b_ref[...] = jnp.where(y > 0.0, y * lax.rsqrt(y), 0.0) * ig * xc

    ns = tt // _SLAB
    rowi = lax.broadcasted_iota(jnp.int32, (_SLAB, LRU_DIM), 0)

    def slab(s, h):
        r0 = pl.multiple_of(((ns - 1 - s) if reverse else s) * _SLAB, _SLAB)
        av = a_ref[pl.ds(r0, _SLAB), :]
        bv = b_ref[pl.ds(r0, _SLAB), :]
        for sh in (1, 2, 4):
            if reverse:
                ok = rowi < _SLAB - sh
                a_sh = pltpu.roll(av, _SLAB - sh, 0)
                b_sh = pltpu.roll(bv, _SLAB - sh, 0)
            else:
                ok = rowi >= sh
                a_sh = pltpu.roll(av, sh, 0)
                b_sh = pltpu.roll(bv, sh, 0)
            bv = jnp.where(ok, av * b_sh + bv, bv)
            av = jnp.where(ok, av * a_sh, av)
        hs = av * h + bv
        hs_ref[pl.ds(r0, _SLAB), :] = hs
        last = hs[0:1] if reverse else hs[_SLAB - 1:_SLAB]
        return jnp.broadcast_to(last, (_SLAB, LRU_DIM))

    hcar_ref[...] = lax.fori_loop(0, ns, slab, hcar_ref[...])
    if combine:
        o_ref[0] = ((hs_ref[...] + hb_ref[0]) * jax.nn.gelu(gt_ref[0])).astype(o_ref.dtype)


def _rglru(xg, h0, conv_w, conv_b, wg, gate_b, lam, reverse, hb=None):
    b, t, _ = xg.shape
    tt = min(256, t)
    nt = t // tt
    combine = hb is not None
    tidx = (lambda i: nt - 1 - i) if reverse else (lambda i: i)
    in_specs = [pl.BlockSpec((1, tt, LRU_DIM), lambda b, i: (b, tidx(i), 0)),
                pl.BlockSpec((1, 1, LRU_DIM), lambda b, i: (b, 0, 0)),
                _const_spec(conv_w.shape), _const_spec((1, LRU_DIM)), _const_spec(wg.shape),
                _const_spec((1, 2 * LRU_DIM)), _const_spec((1, LRU_DIM))]
    args = [xg, h0.reshape(b, 1, LRU_DIM), conv_w, conv_b.reshape(1, -1), wg, gate_b.reshape(1, -1), lam.reshape(1, -1)]
    if combine:
        in_specs += [pl.BlockSpec((1, tt, LRU_DIM), lambda b, i: (b, tidx(i), 0)),
                     pl.BlockSpec((1, tt, LRU_DIM), lambda b, i: (b, tidx(i), 1))]
        args += [hb, xg]
    return pl.pallas_call(
        functools.partial(_lru_kernel, tt=tt, reverse=reverse, combine=combine),
        grid=(b, nt),
        in_specs=in_specs,
        out_specs=pl.BlockSpec((1, tt, LRU_DIM), lambda b, i: (b, tidx(i), 0)),
        out_shape=jax.ShapeDtypeStruct((b, t, LRU_DIM), _BF if combine else _F32),
        scratch_shapes=[pltpu.VMEM((tt + _SLAB, LRU_DIM), _F32),
                        pltpu.VMEM((tt, LRU_DIM), _F32),
                        pltpu.VMEM((tt, LRU_DIM), _F32),
                        pltpu.VMEM((_SLAB, LRU_DIM), _F32),
                        pltpu.VMEM((_SLAB, LRU_DIM), _F32)]
        + ([pltpu.VMEM((tt, LRU_DIM), _F32)] if combine else []),
        compiler_params=_params("parallel", "arbitrary"),
        name="rglru_combine" if combine else ("rglru_rev" if reverse else "rglru_fwd"),
    )(*args)


def _mla_kernel(q_ref, kl_ref, vl_ref, kc_ref, vc_ref, o_ref, m_ref, acc_ref, *, nchunk, tq):
    m_ref[...] = jnp.full(m_ref.shape, NEG_INF, _F32)
    acc_ref[...] = jnp.zeros(acc_ref.shape, _F32)

    def lanes(hd):
        return slice(hd * LANES, (hd + 1) * LANES)

    def vrows(hd):
        return slice(hd * _MLA_VT, (hd + 1) * _MLA_VT)

    def scores(hd, k):
        return _dot_nt(k, q_ref[0, :, lanes(hd)])

    def update(hd, s, vt):
        m = m_ref[hd]
        m_new = jnp.maximum(m, jnp.max(s, axis=0, keepdims=True))
        p = jnp.exp2(s - m_new).astype(_BF)
        acc_ref[hd] = acc_ref[hd] * jnp.exp2(m - m_new) + _dot(vt, p)
        m_ref[hd] = m_new

    def run(units):
        s = {j: scores(units[j][0], units[j][1]()) for j in range(min(_MLA_AHEAD, len(units)))}
        for j, (hd, _, values) in enumerate(units):
            nxt = j + _MLA_AHEAD
            if nxt < len(units):
                s[nxt] = scores(units[nxt][0], units[nxt][1]())
            update(hd, s.pop(j), values())

    unroll = max(u for u in range(1, _MLA_UNROLL + 1) if nchunk % u == 0)

    def latent_units(it):
        units = []
        for sub in range(unroll):
            ci = it * unroll + sub
            r0 = ci * _MLA_TK if isinstance(ci, int) else pl.multiple_of(ci * _MLA_TK, _MLA_TK)
            for c0 in range(0, _MLA_TK, _MLA_SUB):
                units += [(hd, functools.partial(lambda hd, r: kl_ref[0, pl.ds(r, _MLA_SUB), lanes(hd)], hd, r0 + c0),
                           functools.partial(lambda hd, ci, c0: vl_ref[0, ci, vrows(hd), c0:c0 + _MLA_SUB], hd, ci, c0))
                          for hd in range(MLA_HEADS)]
        return units

    context_units = [(hd, functools.partial(lambda hd: kc_ref[0, :, lanes(hd)], hd),
                      functools.partial(lambda hd: vc_ref[0, 0, vrows(hd), :], hd)) for hd in range(MLA_HEADS)]
    niter = nchunk // unroll

    def body(it, carry):
        run(latent_units(it))
        return carry

    lax.fori_loop(0, niter - 1, body, 0)
    run(latent_units(niter - 1) + context_units)
    for pair in range(MLA_HEADS // 2):
        outs = [acc_ref[hd, :MLA_V] / acc_ref[hd, MLA_V:MLA_V + 1] for hd in (2 * pair, 2 * pair + 1)]
        o_ref[0, :, lanes(pair)] = jnp.transpose(jnp.concatenate(outs, axis=0)).astype(_BF)


def _mla_attention(q, k_lat, vt_lat, k_ctx, vt_ctx):
    b, t, _ = q.shape
    lc = k_ctx.shape[1]
    nchunk = vt_lat.shape[1]
    tq = 256
    return pl.pallas_call(
        functools.partial(_mla_kernel, nchunk=nchunk, tq=tq),
        grid=(b, t // tq),
        in_specs=[pl.BlockSpec((1, tq, 1024), lambda b, i: (b, i, 0)),
                  pl.BlockSpec((1, t, 1024), lambda b, i: (b, 0, 0)),
                  pl.BlockSpec((1,) + vt_lat.shape[1:], lambda b, i: (b, 0, 0, 0)),
                  pl.BlockSpec((1, lc, 1024), lambda b, i: (b, 0, 0)),
                  pl.BlockSpec((1,) + vt_ctx.shape[1:], lambda b, i: (b, 0, 0, 0))],
        out_specs=pl.BlockSpec((1, tq, 512), lambda b, i: (b, i, 0)),
        out_shape=jax.ShapeDtypeStruct((b, t, 512), _BF),
        scratch_shapes=[pltpu.VMEM((MLA_HEADS, 1, tq), _F32),
                        pltpu.VMEM((MLA_HEADS, _MLA_VT, tq), _F32)],
        compiler_params=_params("parallel", "parallel"),
        name="mla_attention",
    )(q, k_lat, vt_lat, k_ctx, vt_ctx)


def _rope_tables(t, dim):
    rows = t // GRID_W
    row = np.repeat(np.arange(rows), GRID_W).astype(np.float32)
    col = np.tile(np.arange(GRID_W), rows).astype(np.float32)
    nf = dim // 4
    inv = (np.float32(ROPE_THETA) ** (-np.arange(nf, dtype=np.float32) / np.float32(nf))).astype(np.float32)
    ar = row[:, None] * inv
    ac = col[:, None] * inv
    ang = np.concatenate([ar, ar, ac, ac], axis=-1).astype(np.float32)
    return np.cos(ang).astype(np.float32), np.sin(ang).astype(np.float32)


def _rope_slot_tables(cos, sin, lo, scale=1.0):
    t, dim = cos.shape
    q = dim // 4
    sign = np.concatenate([-np.ones(q), np.zeros(q), -np.ones(q), np.zeros(q)]).astype(np.float32)
    out = np.zeros((3, t, LANES), np.float32)
    out[0] = 1.0
    for l0 in (lo if isinstance(lo, (tuple, list)) else (lo,)):
        out[0, :, l0:l0 + dim] = cos
        out[1, :, l0:l0 + dim] = sin * sign
        out[2, :, l0:l0 + dim] = sin * (1.0 + sign)
    return out * np.float32(scale)


def _ab_weights(w_in, w_out):
    hd = WIN_HEAD_DIM
    g = WIN_HEADS // WIN_KV_HEADS
    glu = w_in[:, :1024]
    wq = w_in[:, 1024:1536]
    wk = w_in[:, 1536:1664]
    wv = w_in[:, 1664:1792]
    zeros = jnp.zeros_like(wk[:, :hd])
    qh = [wq[:, h * hd:(h + 1) * hd] for h in range(WIN_HEADS)]
    kh = [wk[:, h * hd:(h + 1) * hd] for h in range(WIN_KV_HEADS)]
    q_full = jnp.concatenate([jnp.concatenate([qh[j], qh[g + j]], 1) for j in range(g)], 1)
    k_full = jnp.concatenate([kh[0], zeros, zeros, kh[1]], 1)
    w_ext = jnp.concatenate([glu, q_full, k_full, wv], 1).astype(_BF)
    wa = w_out[:CONV_A_DIM].astype(_BF)
    wb_rows = w_out[CONV_A_DIM:]
    wb = jnp.concatenate([jnp.concatenate([wb_rows[j * hd:(j + 1) * hd], wb_rows[(g + j) * hd:(g + j + 1) * hd]], 0)
                          for j in range(g)], 0).astype(_BF)
    return w_ext, wa, wb


def _cd_weights(w_in, w_uq, w_ukv, lat):
    d = w_in.shape[0]
    kr = w_in[:, 1664:1696]
    z32 = jnp.zeros((d, MLA_ROPE), w_in.dtype)
    z64 = jnp.zeros((d, MLA_NOPE), w_in.dtype)
    kr128 = jnp.concatenate([z64, kr, z32] if lat else [z64, z32, kr], 1)
    w1 = jnp.concatenate([w_in[:, :1664], kr128], 1).astype(_BF)
    qd = MLA_NOPE + MLA_ROPE
    full = []
    for h in range(MLA_HEADS):
        nope = w_uq[:, h * qd:h * qd + MLA_NOPE]
        rope = w_uq[:, h * qd + MLA_NOPE:(h + 1) * qd]
        full += [nope, rope, rope]
    wq = jnp.concatenate(full, 1).astype(_BF)
    kd = MLA_NOPE + MLA_V
    zk = jnp.zeros((MLA_KV_RANK, LANES - MLA_NOPE), w_ukv.dtype)
    kcols, vcols = [], []
    for h in range(MLA_HEADS):
        kcols += [w_ukv[:, h * kd:h * kd + MLA_NOPE], zk]
        vcols += [w_ukv[:, h * kd + MLA_NOPE:(h + 1) * kd]]
    wkv = jnp.concatenate(kcols + vcols, 1).astype(_BF)
    return w1, wq, wkv


def _ffn_weights(w_up, conv_w, conv_b, w_down):
    return w_up.astype(_BF), conv_w, conv_b.reshape(1, -1), w_down.astype(_BF)


def _block_diag_dense(w):
    nb, hh, kk = w.shape
    eye = jnp.eye(nb, dtype=w.dtype)
    return (eye[:, None, :, None] * w[:, :, None, :]).reshape(nb * hh, nb * kk)


def kernel(x, c, ctx, c_ctx, w_mod, b_mod, norm_g, ffn_w_up, ffn_conv_w, ffn_conv_b, ffn_w_down, ab_w_in, a_conv_w, a_conv_b, a_ln_g, a_ln_b, b_sink, ab_w_out, cd_w_in, lru_conv_w, lru_conv_b, lru_gate_w, lru_gate_b, lru_lambda, mla_q_norm, mla_w_uq, mla_kv_norm, mla_w_ukv, cd_w_out):
    b, t, d = x.shape
    lc = ctx.shape[1]
    ctx_row = 8 * ((b + 7) // 8)
    cc = jnp.zeros((ctx_row + 8, d), _F32).at[:b].set(c).at[ctx_row].set(c_ctx)
    mods = _modulation(cc, w_mod, b_mod).reshape(w_mod.shape[0], ctx_row + 8, N_MOD, d)

    mod, g = mods[0], norm_g[0]
    w_ext, wa, wb = _ab_weights(ab_w_in[0], ab_w_out[0])
    cos64, sin64 = _rope_tables(t, WIN_HEAD_DIM)
    halves = (0, WIN_HEAD_DIM)
    ident = _rope_slot_tables(np.ones((lc, 4), np.float32), np.zeros((lc, 4), np.float32), 0)
    glu_l, q_l, k_l, v_l = _ab_in(x, mod, None, g, w_ext, _rope_slot_tables(cos64, sin64, halves, _WIN_QSCALE),
                                  _rope_slot_tables(cos64, sin64, halves))
    glu_c, q_c, k_c, v_c = _ab_in(ctx, mod, ctx_row, g, w_ext, ident * np.float32(_WIN_QSCALE), ident)
    a_l = _conformer_conv(glu_l, a_conv_w[0], a_conv_b[0], a_ln_g[0], a_ln_b[0])
    a_c = _conformer_conv(glu_c, a_conv_w[0], a_conv_b[0], a_ln_g[0], a_ln_b[0])
    b_l = _win_attention(q_l, k_l, v_l, k_c, v_c, b_sink[0], True)
    b_c = _win_attention(q_c, k_c, v_c, k_c, v_c, b_sink[0], False)
    xl, hl = _mixer_out(a_l, b_l, wa, wb, x, mod, None, g)
    xc, hc = _mixer_out(a_c, b_c, wa, wb, ctx, mod, ctx_row, g)
    fw = _ffn_weights(ffn_w_up[0], ffn_conv_w[0], ffn_conv_b[0], ffn_w_down[0])
    xl = _ffn(hl, xl, *fw, mod, None, g)
    xc = _ffn(hc, xc, *fw, mod, ctx_row, g)

    mod, g = mods[1], norm_g[1]
    w1_l, wq, wkv = _cd_weights(cd_w_in[0], mla_w_uq[0], mla_w_ukv[0], True)
    w1_c, _, _ = _cd_weights(cd_w_in[0], mla_w_uq[0], mla_w_ukv[0], False)
    cos32, sin32 = _rope_tables(t, MLA_ROPE)
    qn = mla_q_norm[0].reshape(1, -1)
    kvn = mla_kv_norm[0].reshape(1, -1)
    qs = np.float32((MLA_NOPE + MLA_ROPE) ** -0.5 * _LOG2E)
    xg_l, q_l, k_l, v_l = _cd_in(xl, mod, None, g, w1_l, qn, wq, kvn, wkv, _rope_slot_tables(cos32, sin32, MLA_NOPE, qs),
                                 _rope_slot_tables(cos32, sin32, MLA_NOPE))
    xg_c, _, k_c, v_c = _cd_in(xc, mod, ctx_row, g, w1_c, qn, wq, kvn, wkv, ident, ident)
    wg = [jnp.concatenate([_block_diag_dense(lru_gate_w[0, dr, 0]), _block_diag_dense(lru_gate_w[0, dr, 1])], 1).astype(_BF)
          for dr in range(2)]
    gb = [jnp.concatenate([lru_gate_b[0, dr, 0], lru_gate_b[0, dr, 1]]) for dr in range(2)]
    lru = lambda xg, h0, dr, hb=None: _rglru(xg, h0, lru_conv_w[0, dr], lru_conv_b[0, dr], wg[dr], gb[dr],
                                             lru_lambda[0, dr], dr == 1, hb)
    h_zero = jnp.zeros((b, LRU_DIM), _F32)
    hf_c = lru(xg_c, h_zero, 0)
    hb_c = lru(xg_c, h_zero, 1)
    hb_l = lru(xg_l, hb_c[:, 0], 1)
    c_l = lru(xg_l, hf_c[:, lc - 1], 0, hb_l)
    d_l = _mla_attention(q_l, k_l, v_l, k_c, v_c)
    wo = cd_w_out[0]
    xl, hl = _mixer_out(c_l, d_l, wo[:LRU_DIM].astype(_BF), wo[LRU_DIM:].astype(_BF), xl, mod, None, g)
    fw = _ffn_weights(ffn_w_up[1], ffn_conv_w[1], ffn_conv_b[1], ffn_w_down[1])
    return _ffn(hl, xl, *fw, mod, None, g)
```

```python
import functools

import jax
import jax.numpy as jnp
import numpy as np
from jax import lax
from jax.experimental import pallas as pl
from jax.experimental.pallas import tpu as pltpu

_BF = jnp.bfloat16
_F32 = jnp.float32

D_MODEL = 1024
N_MOD = 6
EPS = 1e-6
ROPE_THETA = 10000.0
GRID_W = 64
NEG_INF = -1e30

CONV_A_DIM = 512
CONV_A_WIDTH = 31
WIN_HEADS = 8
WIN_KV_HEADS = 2
WIN_HEAD_DIM = 64
WINDOW = 128
LRU_DIM = 512
LRU_BLOCKS = 8
LRU_CONV_WIDTH = 4
LRU_C = 8.0
MLA_HEADS = 8
MLA_Q_RANK = 384
MLA_KV_RANK = 256
MLA_NOPE = 64
MLA_ROPE = 32
MLA_V = 64
FFN_DIM = 2816
FFN_CHUNK = 256
FFN_NCHUNK = FFN_DIM // FFN_CHUNK
_FFN_DOWN = 1
LANES = 128
_SLAB = 8
HALO = 16
VMEM_LIMIT = 56 * 1024 * 1024
_LOG2E = 1.4426950408889634
_MLA_TK = 512
_MLA_VT = 80
_MLA_SUB = 512
_MLA_AHEAD = 3
_WIN_VT = 80
_WIN_QSCALE = float(np.float32(WIN_HEAD_DIM ** -0.5 * _LOG2E))
_WIN_AHEAD = 3
_MLA_UNROLL = 4


def _params(*sem):
    return pltpu.CompilerParams(dimension_semantics=sem, vmem_limit_bytes=VMEM_LIMIT)


def _const_spec(shape):
    nd = len(shape)
    return pl.BlockSpec(shape, lambda *_: (0,) * nd, pipeline_mode=pl.Buffered(1))


def _rms(x, g):
    return x * lax.rsqrt(jnp.mean(x * x, axis=-1, keepdims=True) + EPS) * g


def _dot(a, b):
    return jnp.dot(a, b, preferred_element_type=_F32)


def _dot_nt(a, b):
    return lax.dot_general(a, b, (((1,), (1,)), ((), ())), preferred_element_type=_F32)


def _rope(x, cos, sin_up, sin_dn, quarter):
    return x * cos + pltpu.roll(x, LANES - quarter, 1) * sin_up + pltpu.roll(x, quarter, 1) * sin_dn


def _mod_spec(mod_row):
    if mod_row is None:
        return pl.BlockSpec((1, N_MOD, D_MODEL), lambda b, i: (b, 0, 0))
    return pl.BlockSpec((1, N_MOD, D_MODEL), lambda b, i: (mod_row, 0, 0))


def _mod_kernel(c_ref, w_ref, b_ref, o_ref):
    c = c_ref[...]
    s = (c * jax.nn.sigmoid(c)).astype(_BF)
    o_ref[0] = _dot(s, w_ref[0].astype(_BF)) + b_ref[0]


def _modulation(cc, w_mod, b_mod):
    depth, d, n = w_mod.shape
    tn = 512
    return pl.pallas_call(
        _mod_kernel,
        grid=(depth, n // tn),
        in_specs=[pl.BlockSpec(cc.shape, lambda l, j: (0, 0)),
                  pl.BlockSpec((1, d, tn), lambda l, j: (l, 0, j)),
                  pl.BlockSpec((1, 1, tn), lambda l, j: (l, 0, j))],
        out_specs=pl.BlockSpec((1, cc.shape[0], tn), lambda l, j: (l, 0, j)),
        out_shape=jax.ShapeDtypeStruct((depth, cc.shape[0], n), _F32),
        compiler_params=_params("parallel", "parallel"),
        name="modulation",
    )(cc, w_mod, b_mod.reshape(depth, 1, n))


_AB_Q0, _AB_K0, _AB_V0, _AB_N = 1024, 1536, 1792, 1920


def _ab_in_kernel(x_ref, mod_ref, g_ref, w_ref, qt_ref, kt_ref, glu_ref, q_ref, k_ref, vt_ref):
    half = x_ref.shape[1] // 2
    rows = [slice(r * half, (r + 1) * half) for r in range(2)]
    zs = [_dot((_rms(x_ref[0, r, :], g_ref[0:1]) * (1.0 + mod_ref[0, 1:2]) + mod_ref[0, 0:1]).astype(_BF), w_ref[...])
          for r in rows]
    quarter = WIN_HEAD_DIM // 4
    ones = jnp.ones((_WIN_VT - WIN_HEAD_DIM, LANES), _BF)
    for hi, (r, z) in enumerate(zip(rows, zs)):
        glu_ref[0, r, :] = z[:, :_AB_Q0]
        qt = [qt_ref[i, r, :] for i in range(3)]
        kt = [kt_ref[i, r, :] for i in range(3)]
        for j in range(4):
            qf = z[:, _AB_Q0 + LANES * j:_AB_Q0 + LANES * (j + 1)]
            q_ref[0, j, r, :] = _rope(qf, *qt, quarter).astype(_BF)
            q_ref[0, 4 + j, r, :] = (qf * _WIN_QSCALE).astype(_BF)
        for j in range(2):
            kf = z[:, _AB_K0 + LANES * j:_AB_K0 + LANES * (j + 1)]
            k_ref[0, r, LANES * j:LANES * (j + 1)] = _rope(kf, *kt, quarter).astype(_BF)
        vt = jnp.transpose(z[:, _AB_V0:_AB_N]).astype(_BF)
        nblk = half // LANES
        for cb in range(nblk):
            for kv in range(WIN_KV_HEADS):
                vt_ref[0, hi * nblk + cb, kv * _WIN_VT:kv * _WIN_VT + WIN_HEAD_DIM, :] = (
                    vt[kv * WIN_HEAD_DIM:(kv + 1) * WIN_HEAD_DIM, cb * LANES:(cb + 1) * LANES])
                vt_ref[0, hi * nblk + cb, kv * _WIN_VT + WIN_HEAD_DIM:(kv + 1) * _WIN_VT, :] = ones


def _ab_in(x, mod, mod_row, g, w_ext, qtab, ktab):
    b, t, d = x.shape
    tm = min(512, t)
    tab = pl.BlockSpec((3, tm, LANES), lambda b, i: (0, i, 0))
    return pl.pallas_call(
        _ab_in_kernel,
        grid=(b, t // tm),
        in_specs=[pl.BlockSpec((1, tm, d), lambda b, i: (b, i, 0)),
                  _mod_spec(mod_row),
                  _const_spec(g.shape),
                  _const_spec(w_ext.shape),
                  tab, tab],
        out_specs=[pl.BlockSpec((1, tm, 1024), lambda b, i: (b, i, 0)),
                   pl.BlockSpec((1, 8, tm, LANES), lambda b, i: (b, 0, i, 0)),
                   pl.BlockSpec((1, tm, 256), lambda b, i: (b, i, 0)),
                   pl.BlockSpec((1, tm // LANES, WIN_KV_HEADS * _WIN_VT, LANES), lambda b, i: (b, i, 0, 0))],
        out_shape=[jax.ShapeDtypeStruct((b, t, 1024), _F32),
                   jax.ShapeDtypeStruct((b, 8, t, LANES), _BF),
                   jax.ShapeDtypeStruct((b, t, 256), _BF),
                   jax.ShapeDtypeStruct((b, t // LANES, WIN_KV_HEADS * _WIN_VT, LANES), _BF)],
        compiler_params=_params("parallel", "parallel"),
        name="ab_in",
    )(x, mod, g, w_ext, qtab, ktab)


_CONV_ROWS = 64


def _conv_kernel(glu_ref, prev_ref, next_ref, w_ref, cb_ref, lg_ref, lb_ref, o_ref, uext_ref, *, tt, nt):
    i = pl.program_id(1)

    def glu(z):
        return z[:, :CONV_A_DIM] * jax.nn.sigmoid(z[:, CONV_A_DIM:])

    uext_ref[0, 0:HALO] = jnp.where(i > 0, glu(prev_ref[0]), 0.0)
    uext_ref[0, HALO:HALO + tt] = glu(glu_ref[0])
    uext_ref[0, HALO + tt:2 * HALO + tt] = jnp.where(i < nt - 1, glu(next_ref[0]), 0.0)
    span = tt + 2 * HALO - _SLAB
    for p in range(1, _SLAB):
        uext_ref[p, 0:span] = uext_ref[0, p:p + span]
    half = CONV_A_WIDTH // 2
    for r in range(tt // _CONV_ROWS):
        acc = jnp.zeros((_CONV_ROWS, CONV_A_DIM), _F32)
        for j in range(CONV_A_WIDTH):
            off = r * _CONV_ROWS + HALO - half + j
            p = off % _SLAB
            acc = acc + w_ref[j:j + 1, :] * uext_ref[p, off - p:off - p + _CONV_ROWS, :]
        u = acc + cb_ref[...]
        mu = jnp.mean(u, axis=-1, keepdims=True)
        var = jnp.mean(jnp.square(u - mu), axis=-1, keepdims=True)
        y = (u - mu) * lax.rsqrt(var + EPS) * lg_ref[...] + lb_ref[...]
        o_ref[0, r * _CONV_ROWS:(r + 1) * _CONV_ROWS, :] = (y * jax.nn.sigmoid(y)).astype(_BF)


def _conformer_conv(glu, conv_w, conv_b, ln_g, ln_b):
    b, t, _ = glu.shape
    tt = min(512, t)
    nt = t // tt
    hb = tt // HALO
    return pl.pallas_call(
        functools.partial(_conv_kernel, tt=tt, nt=nt),
        grid=(b, nt),
        in_specs=[pl.BlockSpec((1, tt, 1024), lambda b, i: (b, i, 0)),
                  pl.BlockSpec((1, HALO, 1024), lambda b, i: (b, jnp.maximum(i * hb - 1, 0), 0)),
                  pl.BlockSpec((1, HALO, 1024), lambda b, i: (b, jnp.minimum((i + 1) * hb, t // HALO - 1), 0)),
                  _const_spec(conv_w.shape),
                  _const_spec((1, CONV_A_DIM)), _const_spec((1, CONV_A_DIM)), _const_spec((1, CONV_A_DIM))],
        out_specs=pl.BlockSpec((1, tt, CONV_A_DIM), lambda b, i: (b, i, 0)),
        out_shape=jax.ShapeDtypeStruct((b, t, CONV_A_DIM), _BF),
        scratch_shapes=[pltpu.VMEM((_SLAB, tt + 2 * HALO, CONV_A_DIM), _F32)],
        compiler_params=_params("parallel", "parallel"),
        name="conformer_conv",
    )(glu, glu, glu, conv_w, conv_b.reshape(1, -1), ln_g.reshape(1, -1), ln_b.reshape(1, -1))


def _win_kernel(sink_ref, q_ref, kl_ref, vl_ref, kc_ref, vc_ref, o_ref, *, t, tq, has_lat):
    n = pl.program_id(1)
    g = WIN_HEADS // WIN_KV_HEADS
    nctx = kc_ref.shape[1] // LANES
    if has_lat:
        band = tq + 2 * WINDOW
        start = pl.multiple_of(jnp.clip(n * tq - WINDOW, 0, t - band), LANES)
        kpos = start + lax.broadcasted_iota(jnp.int32, (band, 1), 0)
        qpos = n * tq + lax.broadcasted_iota(jnp.int32, (1, tq), 1)
        bias = jnp.where(jnp.abs(qpos - kpos) <= WINDOW, 0.0, NEG_INF)

    def kcols(kv):
        return slice(kv * LANES, (kv + 1) * LANES)

    def vrows(kv):
        return slice(kv * _WIN_VT, (kv + 1) * _WIN_VT)

    def scores(kv, j):
        s_ctx = _dot_nt(kc_ref[0, :, kcols(kv)], q_ref[0, (g + j) if has_lat else j])
        if not has_lat:
            return (s_ctx,)
        return s_ctx, _dot_nt(kl_ref[0, pl.ds(start, band), kcols(kv)], q_ref[0, j]) + bias

    def finish(kv, j, s):
        t_sink = jnp.full((1, tq), sink_ref[kv * g + j] * _LOG2E, _F32)
        m = t_sink
        for part in s:
            m = jnp.maximum(m, jnp.max(part, axis=0, keepdims=True))
        vt_ctx = jnp.concatenate([vc_ref[0, cb, vrows(kv), :] for cb in range(nctx)], axis=1)
        acc = _dot(vt_ctx, jnp.exp2(s[0] - m).astype(_BF))
        if has_lat:
            sb = start // LANES
            vt_lat = jnp.concatenate([vl_ref[0, sb + cb, vrows(kv), :] for cb in range(band // LANES)], axis=1)
            acc = acc + _dot(vt_lat, jnp.exp2(s[1] - m).astype(_BF))
        den = acc[WIN_HEAD_DIM:WIN_HEAD_DIM + 1] + jnp.exp2(t_sink - m)
        return acc[:WIN_HEAD_DIM] / den

    units = [(kv, j) for j in range(g) for kv in range(WIN_KV_HEADS)]
    s = {u: scores(*units[u]) for u in range(min(_WIN_AHEAD, len(units)))}
    outs = {}
    for u, (kv, j) in enumerate(units):
        nxt = u + _WIN_AHEAD
        if nxt < len(units):
            s[nxt] = scores(*units[nxt])
        outs[(kv, j)] = finish(kv, j, s.pop(u))
        if kv == WIN_KV_HEADS - 1:
            both = jnp.concatenate([outs.pop((k2, j)) for k2 in range(WIN_KV_HEADS)], axis=0)
            o_ref[0, :, j * LANES:(j + 1) * LANES] = jnp.transpose(both).astype(_BF)


def _win_attention(q, k_lat, vt_lat, k_ctx, vt_ctx, sink, has_lat):
    b, _, t, _ = q.shape
    lc = k_ctx.shape[1]
    tq = 256 if has_lat else t
    tl = k_lat.shape[1]
    return pl.pallas_call(
        functools.partial(_win_kernel, t=t, tq=tq, has_lat=has_lat),
        grid=(b, t // tq),
        in_specs=[pl.BlockSpec(memory_space=pltpu.SMEM),
                  pl.BlockSpec((1, 8, tq, LANES), lambda b, n: (b, 0, n, 0)),
                  pl.BlockSpec((1, tl, 256), lambda b, n: (b, 0, 0)),
                  pl.BlockSpec((1,) + vt_lat.shape[1:], lambda b, n: (b, 0, 0, 0)),
                  pl.BlockSpec((1, lc, 256), lambda b, n: (b, 0, 0)),
                  pl.BlockSpec((1,) + vt_ctx.shape[1:], lambda b, n: (b, 0, 0, 0))],
        out_specs=pl.BlockSpec((1, tq, 512), lambda b, n: (b, n, 0)),
        out_shape=jax.ShapeDtypeStruct((b, t, 512), _BF),
        compiler_params=_params("parallel", "parallel"),
        name="win_attention" if has_lat else "ctx_attention",
    )(sink, q, k_lat, vt_lat, k_ctx, vt_ctx)


def _out_kernel(a_ref, b_ref, wa_ref, wb_ref, x_ref, mod_ref, g_ref, xo_ref):
    half = a_ref.shape[1] // 2
    rows = [slice(r * half, (r + 1) * half) for r in range(2)]
    ys = [_dot(a_ref[0, r, :], wa_ref[...]) + _dot(b_ref[0, r, :], wb_ref[...]) for r in rows]
    for r, y in zip(rows, ys):
        xo_ref[0, r, :] = x_ref[0, r, :] + mod_ref[0, 2:3] * _rms(y, g_ref[1:2])


def _mixer_out(a, bb, wa, wb, x, mod, mod_row, g):
    b, t, d = x.shape
    tm = min(512, t)
    return pl.pallas_call(
        _out_kernel,
        grid=(b, t // tm),
        in_specs=[pl.BlockSpec((1, tm, 512), lambda b, i: (b, i, 0)),
                  pl.BlockSpec((1, tm, 512), lambda b, i: (b, i, 0)),
                  _const_spec(wa.shape), _const_spec(wb.shape),
                  pl.BlockSpec((1, tm, d), lambda b, i: (b, i, 0)),
                  _mod_spec(mod_row),
                  _const_spec(g.shape)],
        out_specs=pl.BlockSpec((1, tm, d), lambda b, i: (b, i, 0)),
        out_shape=jax.ShapeDtypeStruct((b, t, d), _F32),
        compiler_params=_params("parallel", "parallel"),
        name="mixer_out",
    )(a, bb, wa, wb, x, mod, g)


def _ffn_kernel(x_ref, xp_ref, xn_ref, wup_ref, cw_ref, cb_ref, wdn_ref, mod_ref, g_ref,
                o_ref, gbuf_ref, acc_ref, *, tm, nt):
    i = pl.program_id(1)

    def pre_norm(v):
        return (_rms(v, g_ref[2:3]) * (1.0 + mod_ref[0, 4:5]) + mod_ref[0, 3:4]).astype(_BF)

    cur = pre_norm(x_ref[0])
    lhs = jnp.concatenate([pre_norm(xp_ref[0]), cur, pre_norm(xn_ref[0])], axis=0)
    edges = ((HALO - 1, i > 0), (HALO + tm, i < nt - 1))

    def cols(c):
        return slice(c * FFN_CHUNK, (c + 1) * FFN_CHUNK)

    def gate(c):
        return _dot(lhs, wup_ref[:, cols(c)])

    def activation(c, g):
        buf = gbuf_ref.at[c % 2]
        buf[...] = g
        for r, inside in edges:
            buf[r:r + 1, :] = jnp.where(inside, buf[r:r + 1, :], 0.0)
        w = cw_ref[:, cols(c)]
        gc = (w[0:1] * buf[HALO - 1:HALO - 1 + tm, :] + w[1:2] * buf[HALO:HALO + tm, :]
              + w[2:3] * buf[HALO + 1:HALO + 1 + tm, :] + cb_ref[:, cols(c)])
        u = _dot(cur, wup_ref[:, cols(FFN_NCHUNK + c)])
        return (jax.nn.gelu(gc) * u).astype(_BF)

    pending = gate(0)
    acts = []
    for c in range(FFN_NCHUNK):
        following = gate(c + 1) if c + 1 < FFN_NCHUNK else None
        acts.append(activation(c, pending))
        pending = following
        if len(acts) == _FFN_DOWN or c == FFN_NCHUNK - 1:
            c0 = c + 1 - len(acts)
            y = _dot(jnp.concatenate(acts, axis=1), wdn_ref[c0 * FFN_CHUNK:(c + 1) * FFN_CHUNK, :])
            if c0 == 0:
                acc_ref[...] = y
            else:
                acc_ref[...] += y
            acts = []
    o_ref[0] = x_ref[0] + mod_ref[0, 5:6] * _rms(acc_ref[...], g_ref[3:4])


def _ffn(x, wup, cw, cb, wdn, mod, mod_row, g):
    b, t, d = x.shape
    tm = min(512, t)
    nt = t // tm
    hb = tm // HALO
    return pl.pallas_call(
        functools.partial(_ffn_kernel, tm=tm, nt=nt),
        grid=(b, nt),
        in_specs=[pl.BlockSpec((1, tm, d), lambda b, i: (b, i, 0)),
                  pl.BlockSpec((1, HALO, d), lambda b, i: (b, jnp.maximum(i * hb - 1, 0), 0)),
                  pl.BlockSpec((1, HALO, d), lambda b, i: (b, jnp.minimum((i + 1) * hb, t // HALO - 1), 0)),
                  _const_spec(wup.shape), _const_spec(cw.shape), _const_spec(cb.shape), _const_spec(wdn.shape),
                  _mod_spec(mod_row),
                  _const_spec(g.shape)],
        out_specs=pl.BlockSpec((1, tm, d), lambda b, i: (b, i, 0)),
        out_shape=jax.ShapeDtypeStruct((b, t, d), _F32),
        scratch_shapes=[pltpu.VMEM((2, tm + 2 * HALO, FFN_CHUNK), _F32),
                        pltpu.VMEM((tm, d), _F32)],
        compiler_params=_params("parallel", "parallel"),
        name="conv_glu_ffn",
    )(x, x, x, wup, cw, cb, wdn, mod, g)


_CD_CQ0, _CD_CKV0, _CD_KR0, _CD_N = 1024, 1408, 1664, 1792


def _cd_in_kernel(x_ref, mod_ref, g_ref, w1_ref, qn_ref, wq_ref, kvn_ref, wkv_ref,
                  qt_ref, kt_ref, xg_ref, q_ref, k_ref, v_ref):
    half = x_ref.shape[1] // 2
    rows = [slice(r * half, (r + 1) * half) for r in range(2)]
    zs = [_dot((_rms(x_ref[0, r, :], g_ref[0:1]) * (1.0 + mod_ref[0, 1:2]) + mod_ref[0, 0:1]).astype(_BF), w1_ref[...])
          for r in rows]
    ups = [(_dot(_rms(z[:, _CD_CQ0:_CD_CKV0], qn_ref[...]).astype(_BF), wq_ref[...]),
            _dot(_rms(z[:, _CD_CKV0:_CD_KR0], kvn_ref[...]).astype(_BF), wkv_ref[...])) for z in zs]
    quarter = MLA_ROPE // 4
    nq = MLA_HEADS * LANES
    ones = jnp.ones((_MLA_VT - MLA_V, half), _BF)
    for r, z, (qq, kv) in zip(rows, zs, ups):
        xg_ref[0, r, :] = z[:, :_CD_CQ0]
        qt = [qt_ref[i, r, :] for i in range(3)]
        for hd in range(MLA_HEADS):
            lo, hi = hd * LANES, (hd + 1) * LANES
            q_ref[0, r, lo:hi] = _rope(qq[:, lo:hi], *qt, quarter).astype(_BF)
        krr = _rope(z[:, _CD_KR0:_CD_N], *[kt_ref[i, r, :] for i in range(3)], quarter)
        for hd in range(MLA_HEADS):
            lo, hi = hd * LANES, (hd + 1) * LANES
            k_ref[0, r, lo:hi] = (kv[:, lo:hi] + krr).astype(_BF)
        vt = jnp.transpose(kv[:, nq:]).astype(_BF)
        for hd in range(MLA_HEADS):
            v_ref[0, 0, hd * _MLA_VT:hd * _MLA_VT + MLA_V, r] = vt[hd * MLA_V:(hd + 1) * MLA_V]
            v_ref[0, 0, hd * _MLA_VT + MLA_V:(hd + 1) * _MLA_VT, r] = ones


def _cd_in(x, mod, mod_row, g, w1, qn, wq, kvn, wkv, qtab, ktab):
    b, t, d = x.shape
    tm = min(512, t)
    tab = pl.BlockSpec((3, tm, LANES), lambda b, i: (0, i, 0))
    return pl.pallas_call(
        _cd_in_kernel,
        grid=(b, t // tm),
        in_specs=[pl.BlockSpec((1, tm, d), lambda b, i: (b, i, 0)),
                  _mod_spec(mod_row),
                  _const_spec(g.shape), _const_spec(w1.shape), _const_spec(qn.shape), _const_spec(wq.shape),
                  _const_spec(kvn.shape), _const_spec(wkv.shape), tab, tab],
        out_specs=[pl.BlockSpec((1, tm, 1024), lambda b, i: (b, i, 0)),
                   pl.BlockSpec((1, tm, 1024), lambda b, i: (b, i, 0)),
                   pl.BlockSpec((1, tm, 1024), lambda b, i: (b, i, 0)),
                   pl.BlockSpec((1, 1, MLA_HEADS * _MLA_VT, tm), lambda b, i: (b, i, 0, 0))],
        out_shape=[jax.ShapeDtypeStruct((b, t, 1024), _F32),
                   jax.ShapeDtypeStruct((b, t, 1024), _BF),
                   jax.ShapeDtypeStruct((b, t, 1024), _BF),
                   jax.ShapeDtypeStruct((b, t // tm, MLA_HEADS * _MLA_VT, tm), _BF)],
        compiler_params=_params("parallel", "parallel"),
        name="cd_in",
    )(x, mod, g, w1, qn, wq, kvn, wkv, qtab, ktab)


def _lru_kernel(*refs, tt, reverse, combine):
    if combine:
        (x_ref, h0_ref, cw_ref, cb_ref, wg_ref, gb_ref, lam_ref, hb_ref, gt_ref,
         o_ref, xext_ref, a_ref, b_ref, hcar_ref, xcar_ref, hs_ref) = refs
    else:
        (x_ref, h0_ref, cw_ref, cb_ref, wg_ref, gb_ref, lam_ref,
         o_ref, xext_ref, a_ref, b_ref, hcar_ref, xcar_ref) = refs
        hs_ref = o_ref.at[0]
    i = pl.program_id(1)

    @pl.when(i == 0)
    def _():
        hcar_ref[...] = jnp.broadcast_to(h0_ref[0], (_SLAB, LRU_DIM))
        xcar_ref[...] = jnp.zeros_like(xcar_ref)

    x = x_ref[0]
    if reverse:
        xext_ref[0:tt] = x
        xext_ref[tt:tt + _SLAB] = xcar_ref[...]
        xcar_ref[...] = x[0:_SLAB]
        base = 0
    else:
        xext_ref[0:_SLAB] = xcar_ref[...]
        xext_ref[_SLAB:_SLAB + tt] = x
        xcar_ref[...] = x[tt - _SLAB:tt]
        base = _SLAB - (LRU_CONV_WIDTH - 1)
    xc = cb_ref[...]
    for j in range(LRU_CONV_WIDTH):
        xc = xc + cw_ref[j:j + 1, :] * xext_ref[base + j:base + j + tt, :]
    gates = _dot(xc.astype(_BF), wg_ref[...]) + gb_ref[...]
    r = jax.nn.sigmoid(gates[:, :LRU_DIM])
    ig = jax.nn.sigmoid(gates[:, LRU_DIM:])
    nl = -lam_ref[...]
    neg_sp = jnp.maximum(nl, 0.0) + jnp.log1p(jnp.exp(-jnp.abs(nl)))
    log_a = -LRU_C * r * neg_sp
    th = jnp.tanh(log_a)
    a_ref[...] = jnp.exp(log_a)
    y = -2.0 * th / (1.0 - th)
    b_ref[...] = jnp.where(y > 0.0, y * lax.rsqrt(y), 0.0) * ig * xc

    ns = tt // _SLAB
    rowi = lax.broadcasted_iota(jnp.int32, (_SLAB, LRU_DIM), 0)

    def slab(s, h):
        r0 = pl.multiple_of(((ns - 1 - s) if reverse else s) * _SLAB, _SLAB)
        av = a_ref[pl.ds(r0, _SLAB), :]
        bv = b_ref[pl.ds(r0, _SLAB), :]
        for sh in (1, 2, 4):
            if reverse:
                ok = rowi < _SLAB - sh
                a_sh = pltpu.roll(av, _SLAB - sh, 0)
                b_sh = pltpu.roll(bv, _SLAB - sh, 0)
            else:
                ok = rowi >= sh
                a_sh = pltpu.roll(av, sh, 0)
                b_sh = pltpu.roll(bv, sh, 0)
            bv = jnp.where(ok, av * b_sh + bv, bv)
            av = jnp.where(ok, av * a_sh, av)
        hs = av * h + bv
        hs_ref[pl.ds(r0, _SLAB), :] = hs
        last = hs[0:1] if reverse else hs[_SLAB - 1:_SLAB]
        return jnp.broadcast_to(last, (_SLAB, LRU_DIM))

    hcar_ref[...] = lax.fori_loop(0, ns, slab, hcar_ref[...])
    if combine:
        o_ref[0] = ((hs_ref[...] + hb_ref[0]) * jax.nn.gelu(gt_ref[0])).astype(o_ref.dtype)


def _rglru(xg, h0, conv_w, conv_b, wg, gate_b, lam, reverse, hb=None):
    b, t, _ = xg.shape
    tt = min(256, t)
    nt = t // tt
    combine = hb is not None
    tidx = (lambda i: nt - 1 - i) if reverse else (lambda i: i)
    in_specs = [pl.BlockSpec((1, tt, LRU_DIM), lambda b, i: (b, tidx(i), 0)),
                pl.BlockSpec((1, 1, LRU_DIM), lambda b, i: (b, 0, 0)),
                _const_spec(conv_w.shape), _const_spec((1, LRU_DIM)), _const_spec(wg.shape),
                _const_spec((1, 2 * LRU_DIM)), _const_spec((1, LRU_DIM))]
    args = [xg, h0.reshape(b, 1, LRU_DIM), conv_w, conv_b.reshape(1, -1), wg, gate_b.reshape(1, -1), lam.reshape(1, -1)]
    if combine:
        in_specs += [pl.BlockSpec((1, tt, LRU_DIM), lambda b, i: (b, tidx(i), 0)),
                     pl.BlockSpec((1, tt, LRU_DIM), lambda b, i: (b, tidx(i), 1))]
        args += [hb, xg]
    return pl.pallas_call(
        functools.partial(_lru_kernel, tt=tt, reverse=reverse, combine=combine),
        grid=(b, nt),
        in_specs=in_specs,
        out_specs=pl.BlockSpec((1, tt, LRU_DIM), lambda b, i: (b, tidx(i), 0)),
        out_shape=jax.ShapeDtypeStruct((b, t, LRU_DIM), _BF if combine else _F32),
        scratch_shapes=[pltpu.VMEM((tt + _SLAB, LRU_DIM), _F32),
                        pltpu.VMEM((tt, LRU_DIM), _F32),
                        pltpu.VMEM((tt, LRU_DIM), _F32),
                        pltpu.VMEM((_SLAB, LRU_DIM), _F32),
                        pltpu.VMEM((_SLAB, LRU_DIM), _F32)]
        + ([pltpu.VMEM((tt, LRU_DIM), _F32)] if combine else []),
        compiler_params=_params("parallel", "arbitrary"),
        name="rglru_combine" if combine else ("rglru_rev" if reverse else "rglru_fwd"),
    )(*args)


def _mla_kernel(q_ref, kl_ref, vl_ref, kc_ref, vc_ref, o_ref, m_ref, acc_ref, *, nchunk, tq):
    m_ref[...] = jnp.full(m_ref.shape, NEG_INF, _F32)
    acc_ref[...] = jnp.zeros(acc_ref.shape, _F32)

    def lanes(hd):
        return slice(hd * LANES, (hd + 1) * LANES)

    def vrows(hd):
        return slice(hd * _MLA_VT, (hd + 1) * _MLA_VT)

    def scores(hd, k):
        return _dot_nt(k, q_ref[0, :, lanes(hd)])

    def update(hd, s, vt):
        m = m_ref[hd]
        m_new = jnp.maximum(m, jnp.max(s, axis=0, keepdims=True))
        p = jnp.exp2(s - m_new).astype(_BF)
        acc_ref[hd] = acc_ref[hd] * jnp.exp2(m - m_new) + _dot(vt, p)
        m_ref[hd] = m_new

    def run(units):
        s = {j: scores(units[j][0], units[j][1]()) for j in range(min(_MLA_AHEAD, len(units)))}
        for j, (hd, _, values) in enumerate(units):
            nxt = j + _MLA_AHEAD
            if nxt < len(units):
                s[nxt] = scores(units[nxt][0], units[nxt][1]())
            update(hd, s.pop(j), values())

    unroll = max(u for u in range(1, _MLA_UNROLL + 1) if nchunk % u == 0)

    def latent_units(it):
        units = []
        for sub in range(unroll):
            ci = it * unroll + sub
            r0 = ci * _MLA_TK if isinstance(ci, int) else pl.multiple_of(ci * _MLA_TK, _MLA_TK)
            for c0 in range(0, _MLA_TK, _MLA_SUB):
                units += [(hd, functools.partial(lambda hd, r: kl_ref[0, pl.ds(r, _MLA_SUB), lanes(hd)], hd, r0 + c0),
                           functools.partial(lambda hd, ci, c0: vl_ref[0, ci, vrows(hd), c0:c0 + _MLA_SUB], hd, ci, c0))
                          for hd in range(MLA_HEADS)]
        return units

    context_units = [(hd, functools.partial(lambda hd: kc_ref[0, :, lanes(hd)], hd),
                      functools.partial(lambda hd: vc_ref[0, 0, vrows(hd), :], hd)) for hd in range(MLA_HEADS)]
    niter = nchunk // unroll

    def body(it, carry):
        run(latent_units(it))
        return carry

    lax.fori_loop(0, niter - 1, body, 0)
    run(latent_units(niter - 1) + context_units)
    for pair in range(MLA_HEADS // 2):
        outs = [acc_ref[hd, :MLA_V] / acc_ref[hd, MLA_V:MLA_V + 1] for hd in (2 * pair, 2 * pair + 1)]
        o_ref[0, :, lanes(pair)] = jnp.transpose(jnp.concatenate(outs, axis=0)).astype(_BF)


def _mla_attention(q, k_lat, vt_lat, k_ctx, vt_ctx):
    b, t, _ = q.shape
    lc = k_ctx.shape[1]
    nchunk = vt_lat.shape[1]
    tq = 256
    return pl.pallas_call(
        functools.partial(_mla_kernel, nchunk=nchunk, tq=tq),
        grid=(b, t // tq),
        in_specs=[pl.BlockSpec((1, tq, 1024), lambda b, i: (b, i, 0)),
                  pl.BlockSpec((1, t, 1024), lambda b, i: (b, 0, 0)),
                  pl.BlockSpec((1,) + vt_lat.shape[1:], lambda b, i: (b, 0, 0, 0)),
                  pl.BlockSpec((1, lc, 1024), lambda b, i: (b, 0, 0)),
                  pl.BlockSpec((1,) + vt_ctx.shape[1:], lambda b, i: (b, 0, 0, 0))],
        out_specs=pl.BlockSpec((1, tq, 512), lambda b, i: (b, i, 0)),
        out_shape=jax.ShapeDtypeStruct((b, t, 512), _BF),
        scratch_shapes=[pltpu.VMEM((MLA_HEADS, 1, tq), _F32),
                        pltpu.VMEM((MLA_HEADS, _MLA_VT, tq), _F32)],
        compiler_params=_params("parallel", "parallel"),
        name="mla_attention",
    )(q, k_lat, vt_lat, k_ctx, vt_ctx)


def _rope_tables(t, dim):
    rows = t // GRID_W
    row = np.repeat(np.arange(rows), GRID_W).astype(np.float32)
    col = np.tile(np.arange(GRID_W), rows).astype(np.float32)
    nf = dim // 4
    inv = (np.float32(ROPE_THETA) ** (-np.arange(nf, dtype=np.float32) / np.float32(nf))).astype(np.float32)
    ar = row[:, None] * inv
    ac = col[:, None] * inv
    ang = np.concatenate([ar, ar, ac, ac], axis=-1).astype(np.float32)
    return np.cos(ang).astype(np.float32), np.sin(ang).astype(np.float32)


def _rope_slot_tables(cos, sin, lo, scale=1.0):
    t, dim = cos.shape
    q = dim // 4
    sign = np.concatenate([-np.ones(q), np.zeros(q), -np.ones(q), np.zeros(q)]).astype(np.float32)
    out = np.zeros((3, t, LANES), np.float32)
    out[0] = 1.0
    for l0 in (lo if isinstance(lo, (tuple, list)) else (lo,)):
        out[0, :, l0:l0 + dim] = cos
        out[1, :, l0:l0 + dim] = sin * sign
        out[2, :, l0:l0 + dim] = sin * (1.0 + sign)
    return out * np.float32(scale)


def _ab_weights(w_in, w_out):
    hd = WIN_HEAD_DIM
    g = WIN_HEADS // WIN_KV_HEADS
    glu = w_in[:, :1024]
    wq = w_in[:, 1024:1536]
    wk = w_in[:, 1536:1664]
    wv = w_in[:, 1664:1792]
    zeros = jnp.zeros_like(wk[:, :hd])
    qh = [wq[:, h * hd:(h + 1) * hd] for h in range(WIN_HEADS)]
    kh = [wk[:, h * hd:(h + 1) * hd] for h in range(WIN_KV_HEADS)]
    q_full = jnp.concatenate([jnp.concatenate([qh[j], qh[g + j]], 1) for j in range(g)], 1)
    k_full = jnp.concatenate([kh[0], zeros, zeros, kh[1]], 1)
    w_ext = jnp.concatenate([glu, q_full, k_full, wv], 1).astype(_BF)
    wa = w_out[:CONV_A_DIM].astype(_BF)
    wb_rows = w_out[CONV_A_DIM:]
    wb = jnp.concatenate([jnp.concatenate([wb_rows[j * hd:(j + 1) * hd], wb_rows[(g + j) * hd:(g + j + 1) * hd]], 0)
                          for j in range(g)], 0).astype(_BF)
    return w_ext, wa, wb


def _cd_weights(w_in, w_uq, w_ukv, lat):
    d = w_in.shape[0]
    kr = w_in[:, 1664:1696]
    z32 = jnp.zeros((d, MLA_ROPE), w_in.dtype)
    z64 = jnp.zeros((d, MLA_NOPE), w_in.dtype)
    kr128 = jnp.concatenate([z64, kr, z32] if lat else [z64, z32, kr], 1)
    w1 = jnp.concatenate([w_in[:, :1664], kr128], 1).astype(_BF)
    qd = MLA_NOPE + MLA_ROPE
    full = []
    for h in range(MLA_HEADS):
        nope = w_uq[:, h * qd:h * qd + MLA_NOPE]
        rope = w_uq[:, h * qd + MLA_NOPE:(h + 1) * qd]
        full += [nope, rope, rope]
    wq = jnp.concatenate(full, 1).astype(_BF)
    kd = MLA_NOPE + MLA_V
    zk = jnp.zeros((MLA_KV_RANK, LANES - MLA_NOPE), w_ukv.dtype)
    kcols, vcols = [], []
    for h in range(MLA_HEADS):
        kcols += [w_ukv[:, h * kd:h * kd + MLA_NOPE], zk]
        vcols += [w_ukv[:, h * kd + MLA_NOPE:(h + 1) * kd]]
    wkv = jnp.concatenate(kcols + vcols, 1).astype(_BF)
    return w1, wq, wkv


def _ffn_weights(w_up, conv_w, conv_b, w_down):
    return w_up.astype(_BF), conv_w, conv_b.reshape(1, -1), w_down.astype(_BF)


def _block_diag_dense(w):
    nb, hh, kk = w.shape
    eye = jnp.eye(nb, dtype=w.dtype)
    return (eye[:, None, :, None] * w[:, :, None, :]).reshape(nb * hh, nb * kk)


def kernel(x, c, ctx, c_ctx, w_mod, b_mod, norm_g, ffn_w_up, ffn_conv_w, ffn_conv_b, ffn_w_down, ab_w_in, a_conv_w, a_conv_b, a_ln_g, a_ln_b, b_sink, ab_w_out, cd_w_in, lru_conv_w, lru_conv_b, lru_gate_w, lru_gate_b, lru_lambda, mla_q_norm, mla_w_uq, mla_kv_norm, mla_w_ukv, cd_w_out):
    b, t, d = x.shape
    lc = ctx.shape[1]
    ctx_row = 8 * ((b + 7) // 8)
    cc = jnp.zeros((ctx_row + 8, d), _F32).at[:b].set(c).at[ctx_row].set(c_ctx)
    mods = _modulation(cc, w_mod, b_mod).reshape(w_mod.shape[0], ctx_row + 8, N_MOD, d)

    mod, g = mods[0], norm_g[0]
    w_ext, wa, wb = _ab_weights(ab_w_in[0], ab_w_out[0])
    cos64, sin64 = _rope_tables(t, WIN_HEAD_DIM)
    halves = (0, WIN_HEAD_DIM)
    ident = _rope_slot_tables(np.ones((lc, 4), np.float32), np.zeros((lc, 4), np.float32), 0)
    glu_l, q_l, k_l, v_l = _ab_in(x, mod, None, g, w_ext, _rope_slot_tables(cos64, sin64, halves, _WIN_QSCALE),
                                  _rope_slot_tables(cos64, sin64, halves))
    glu_c, q_c, k_c, v_c = _ab_in(ctx, mod, ctx_row, g, w_ext, ident * np.float32(_WIN_QSCALE), ident)
    a_l = _conformer_conv(glu_l, a_conv_w[0], a_conv_b[0], a_ln_g[0], a_ln_b[0])
    a_c = _conformer_conv(glu_c, a_conv_w[0], a_conv_b[0], a_ln_g[0], a_ln_b[0])
    b_l = _win_attention(q_l, k_l, v_l, k_c, v_c, b_sink[0], True)
    b_c = _win_attention(q_c, k_c, v_c, k_c, v_c, b_sink[0], False)
    xl = _mixer_out(a_l, b_l, wa, wb, x, mod, None, g)
    xc = _mixer_out(a_c, b_c, wa, wb, ctx, mod, ctx_row, g)
    fw = _ffn_weights(ffn_w_up[0], ffn_conv_w[0], ffn_conv_b[0], ffn_w_down[0])
    xl = _ffn(xl, *fw, mod, None, g)
    xc = _ffn(xc, *fw, mod, ctx_row, g)

    mod, g = mods[1], norm_g[1]
    w1_l, wq, wkv = _cd_weights(cd_w_in[0], mla_w_uq[0], mla_w_ukv[0], True)
    w1_c, _, _ = _cd_weights(cd_w_in[0], mla_w_uq[0], mla_w_ukv[0], False)
    cos32, sin32 = _rope_tables(t, MLA_ROPE)
    qn = mla_q_norm[0].reshape(1, -1)
    kvn = mla_kv_norm[0].reshape(1, -1)
    qs = np.float32((MLA_NOPE + MLA_ROPE) ** -0.5 * _LOG2E)
    xg_l, q_l, k_l, v_l = _cd_in(xl, mod, None, g, w1_l, qn, wq, kvn, wkv, _rope_slot_tables(cos32, sin32, MLA_NOPE, qs),
                                 _rope_slot_tables(cos32, sin32, MLA_NOPE))
    xg_c, _, k_c, v_c = _cd_in(xc, mod, ctx_row, g, w1_c, qn, wq, kvn, wkv, ident, ident)
    wg = [jnp.concatenate([_block_diag_dense(lru_gate_w[0, dr, 0]), _block_diag_dense(lru_gate_w[0, dr, 1])], 1).astype(_BF)
          for dr in range(2)]
    gb = [jnp.concatenate([lru_gate_b[0, dr, 0], lru_gate_b[0, dr, 1]]) for dr in range(2)]
    lru = lambda xg, h0, dr, hb=None: _rglru(xg, h0, lru_conv_w[0, dr], lru_conv_b[0, dr], wg[dr], gb[dr],
                                             lru_lambda[0, dr], dr == 1, hb)
    h_zero = jnp.zeros((b, LRU_DIM), _F32)
    hf_c = lru(xg_c, h_zero, 0)
    hb_c = lru(xg_c, h_zero, 1)
    hb_l = lru(xg_l, hb_c[:, 0], 1)
    c_l = lru(xg_l, hf_c[:, lc - 1], 0, hb_l)
    d_l = _mla_attention(q_l, k_l, v_l, k_c, v_c)
    wo = cd_w_out[0]
    xl = _mixer_out(c_l, d_l, wo[:LRU_DIM].astype(_BF), wo[LRU_DIM:].astype(_BF), xl, mod, None, g)
    fw = _ffn_weights(ffn_w_up[1], ffn_conv_w[1], ffn_conv_b[1], ffn_w_down[1])
    return _ffn(xl, *fw, mod, None, g)
```

```python
import functools

import jax
import jax.numpy as jnp
import numpy as np
from jax import lax
from jax.experimental import pallas as pl
from jax.experimental.pallas import tpu as pltpu

_BF = jnp.bfloat16
_F32 = jnp.float32

D_MODEL = 1024
N_MOD = 6
EPS = 1e-6
ROPE_THETA = 10000.0
GRID_W = 64
NEG_INF = -1e30

CONV_A_DIM = 512
CONV_A_WIDTH = 31
WIN_HEADS = 8
WIN_KV_HEADS = 2
WIN_HEAD_DIM = 64
WINDOW = 128
LRU_DIM = 512
LRU_BLOCKS = 8
LRU_CONV_WIDTH = 4
LRU_C = 8.0
MLA_HEADS = 8
MLA_Q_RANK = 384
MLA_KV_RANK = 256
MLA_NOPE = 64
MLA_ROPE = 32
MLA_V = 64
FFN_DIM = 2816
FFN_CHUNK = 256
FFN_NCHUNK = FFN_DIM // FFN_CHUNK
_FFN_DOWN = 1
LANES = 128
_SLAB = 8
HALO = 16
VMEM_LIMIT = 56 * 1024 * 1024
_LOG2E = 1.4426950408889634
_MLA_TK = 512
_MLA_VT = 80
_MLA_SUB = 512
_MLA_AHEAD = 3
_WIN_VT = 80
_WIN_QSCALE = float(np.float32(WIN_HEAD_DIM ** -0.5 * _LOG2E))
_WIN_AHEAD = 3
_MLA_UNROLL = 4


def _params(*sem):
    return pltpu.CompilerParams(dimension_semantics=sem, vmem_limit_bytes=VMEM_LIMIT)


def _const_spec(shape):
    nd = len(shape)
    return pl.BlockSpec(shape, lambda *_: (0,) * nd, pipeline_mode=pl.Buffered(1))


def _rms(x, g):
    return x * lax.rsqrt(jnp.mean(x * x, axis=-1, keepdims=True) + EPS) * g


def _dot(a, b):
    return jnp.dot(a, b, preferred_element_type=_F32)


def _dot_nt(a, b):
    return lax.dot_general(a, b, (((1,), (1,)), ((), ())), preferred_element_type=_F32)


def _rope(x, cos, sin_up, sin_dn, quarter):
    return x * cos + pltpu.roll(x, LANES - quarter, 1) * sin_up + pltpu.roll(x, quarter, 1) * sin_dn


def _mod_spec(mod_row):
    if mod_row is None:
        return pl.BlockSpec((1, N_MOD, D_MODEL), lambda b, i: (b, 0, 0))
    return pl.BlockSpec((1, N_MOD, D_MODEL), lambda b, i: (mod_row, 0, 0))


def _mod_kernel(c_ref, w_ref, b_ref, o_ref):
    c = c_ref[...]
    s = (c * jax.nn.sigmoid(c)).astype(_BF)
    o_ref[0] = _dot(s, w_ref[0].astype(_BF)) + b_ref[0]


def _modulation(cc, w_mod, b_mod):
    depth, d, n = w_mod.shape
    tn = 512
    return pl.pallas_call(
        _mod_kernel,
        grid=(depth, n // tn),
        in_specs=[pl.BlockSpec(cc.shape, lambda l, j: (0, 0)),
                  pl.BlockSpec((1, d, tn), lambda l, j: (l, 0, j)),
                  pl.BlockSpec((1, 1, tn), lambda l, j: (l, 0, j))],
        out_specs=pl.BlockSpec((1, cc.shape[0], tn), lambda l, j: (l, 0, j)),
        out_shape=jax.ShapeDtypeStruct((depth, cc.shape[0], n), _F32),
        compiler_params=_params("parallel", "parallel"),
        name="modulation",
    )(cc, w_mod, b_mod.reshape(depth, 1, n))


_AB_Q0, _AB_K0, _AB_V0, _AB_N = 1024, 1536, 1792, 1920


def _ab_in_kernel(x_ref, mod_ref, g_ref, w_ref, qt_ref, kt_ref, glu_ref, q_ref, k_ref, vt_ref):
    half = x_ref.shape[1] // 2
    rows = [slice(r * half, (r + 1) * half) for r in range(2)]
    zs = [_dot((_rms(x_ref[0, r, :], g_ref[0:1]) * (1.0 + mod_ref[0, 1:2]) + mod_ref[0, 0:1]).astype(_BF), w_ref[...])
          for r in rows]
    quarter = WIN_HEAD_DIM // 4
    ones = jnp.ones((_WIN_VT - WIN_HEAD_DIM, LANES), _BF)
    for hi, (r, z) in enumerate(zip(rows, zs)):
        glu_ref[0, r, :] = z[:, :_AB_Q0]
        qt = [qt_ref[i, r, :] for i in range(3)]
        kt = [kt_ref[i, r, :] for i in range(3)]
        for j in range(4):
            qf = z[:, _AB_Q0 + LANES * j:_AB_Q0 + LANES * (j + 1)]
            q_ref[0, j, r, :] = _rope(qf, *qt, quarter).astype(_BF)
            q_ref[0, 4 + j, r, :] = (qf * _WIN_QSCALE).astype(_BF)
        for j in range(2):
            kf = z[:, _AB_K0 + LANES * j:_AB_K0 + LANES * (j + 1)]
            k_ref[0, r, LANES * j:LANES * (j + 1)] = _rope(kf, *kt, quarter).astype(_BF)
        vt = jnp.transpose(z[:, _AB_V0:_AB_N]).astype(_BF)
        nblk = half // LANES
        for cb in range(nblk):
            for kv in range(WIN_KV_HEADS):
                vt_ref[0, hi * nblk + cb, kv * _WIN_VT:kv * _WIN_VT + WIN_HEAD_DIM, :] = (
                    vt[kv * WIN_HEAD_DIM:(kv + 1) * WIN_HEAD_DIM, cb * LANES:(cb + 1) * LANES])
                vt_ref[0, hi * nblk + cb, kv * _WIN_VT + WIN_HEAD_DIM:(kv + 1) * _WIN_VT, :] = ones


def _ab_in(x, mod, mod_row, g, w_ext, qtab, ktab):
    b, t, d = x.shape
    tm = min(512, t)
    tab = pl.BlockSpec((3, tm, LANES), lambda b, i: (0, i, 0))
    return pl.pallas_call(
        _ab_in_kernel,
        grid=(b, t // tm),
        in_specs=[pl.BlockSpec((1, tm, d), lambda b, i: (b, i, 0)),
                  _mod_spec(mod_row),
                  _const_spec(g.shape),
                  _const_spec(w_ext.shape),
                  tab, tab],
        out_specs=[pl.BlockSpec((1, tm, 1024), lambda b, i: (b, i, 0)),
                   pl.BlockSpec((1, 8, tm, LANES), lambda b, i: (b, 0, i, 0)),
                   pl.BlockSpec((1, tm, 256), lambda b, i: (b, i, 0)),
                   pl.BlockSpec((1, tm // LANES, WIN_KV_HEADS * _WIN_VT, LANES), lambda b, i: (b, i, 0, 0))],
        out_shape=[jax.ShapeDtypeStruct((b, t, 1024), _F32),
                   jax.ShapeDtypeStruct((b, 8, t, LANES), _BF),
                   jax.ShapeDtypeStruct((b, t, 256), _BF),
                   jax.ShapeDtypeStruct((b, t // LANES, WIN_KV_HEADS * _WIN_VT, LANES), _BF)],
        compiler_params=_params("parallel", "parallel"),
        name="ab_in",
    )(x, mod, g, w_ext, qtab, ktab)


_CONV_ROWS = 64


def _conv_kernel(glu_ref, prev_ref, next_ref, w_ref, cb_ref, lg_ref, lb_ref, o_ref, uext_ref, *, tt, nt):
    i = pl.program_id(1)

    def glu(z):
        return z[:, :CONV_A_DIM] * jax.nn.sigmoid(z[:, CONV_A_DIM:])

    uext_ref[0, 0:HALO] = jnp.where(i > 0, glu(prev_ref[0]), 0.0)
    uext_ref[0, HALO:HALO + tt] = glu(glu_ref[0])
    uext_ref[0, HALO + tt:2 * HALO + tt] = jnp.where(i < nt - 1, glu(next_ref[0]), 0.0)
    span = tt + 2 * HALO - _SLAB
    for p in range(1, _SLAB):
        uext_ref[p, 0:span] = uext_ref[0, p:p + span]
    half = CONV_A_WIDTH // 2
    for r in range(tt // _CONV_ROWS):
        acc = jnp.zeros((_CONV_ROWS, CONV_A_DIM), _F32)
        for j in range(CONV_A_WIDTH):
            off = r * _CONV_ROWS + HALO - half + j
            p = off % _SLAB
            acc = acc + w_ref[j:j + 1, :] * uext_ref[p, off - p:off - p + _CONV_ROWS, :]
        u = acc + cb_ref[...]
        mu = jnp.mean(u, axis=-1, keepdims=True)
        var = jnp.mean(jnp.square(u - mu), axis=-1, keepdims=True)
        y = (u - mu) * lax.rsqrt(var + EPS) * lg_ref[...] + lb_ref[...]
        o_ref[0, r * _CONV_ROWS:(r + 1) * _CONV_ROWS, :] = (y * jax.nn.sigmoid(y)).astype(_BF)


def _conformer_conv(glu, conv_w, conv_b, ln_g, ln_b):
    b, t, _ = glu.shape
    tt = min(512, t)
    nt = t // tt
    hb = tt // HALO
    return pl.pallas_call(
        functools.partial(_conv_kernel, tt=tt, nt=nt),
        grid=(b, nt),
        in_specs=[pl.BlockSpec((1, tt, 1024), lambda b, i: (b, i, 0)),
                  pl.BlockSpec((1, HALO, 1024), lambda b, i: (b, jnp.maximum(i * hb - 1, 0), 0)),
                  pl.BlockSpec((1, HALO, 1024), lambda b, i: (b, jnp.minimum((i + 1) * hb, t // HALO - 1), 0)),
                  _const_spec(conv_w.shape),
                  _const_spec((1, CONV_A_DIM)), _const_spec((1, CONV_A_DIM)), _const_spec((1, CONV_A_DIM))],
        out_specs=pl.BlockSpec((1, tt, CONV_A_DIM), lambda b, i: (b, i, 0)),
        out_shape=jax.ShapeDtypeStruct((b, t, CONV_A_DIM), _BF),
        scratch_shapes=[pltpu.VMEM((_SLAB, tt + 2 * HALO, CONV_A_DIM), _F32)],
        compiler_params=_params("parallel", "parallel"),
        name="conformer_conv",
    )(glu, glu, glu, conv_w, conv_b.reshape(1, -1), ln_g.reshape(1, -1), ln_b.reshape(1, -1))


def _win_kernel(sink_ref, q_ref, kl_ref, vl_ref, kc_ref, vc_ref, o_ref, *, t, tq, has_lat):
    n = pl.program_id(1)
    g = WIN_HEADS // WIN_KV_HEADS
    nctx = kc_ref.shape[1] // LANES
    if has_lat:
        band = tq + 2 * WINDOW
        start = pl.multiple_of(jnp.clip(n * tq - WINDOW, 0, t - band), LANES)
        kpos = start + lax.broadcasted_iota(jnp.int32, (band, 1), 0)
        qpos = n * tq + lax.broadcasted_iota(jnp.int32, (1, tq), 1)
        bias = jnp.where(jnp.abs(qpos - kpos) <= WINDOW, 0.0, NEG_INF)

    def kcols(kv):
        return slice(kv * LANES, (kv + 1) * LANES)

    def vrows(kv):
        return slice(kv * _WIN_VT, (kv + 1) * _WIN_VT)

    def scores(kv, j):
        s_ctx = _dot_nt(kc_ref[0, :, kcols(kv)], q_ref[0, (g + j) if has_lat else j])
        if not has_lat:
            return (s_ctx,)
        return s_ctx, _dot_nt(kl_ref[0, pl.ds(start, band), kcols(kv)], q_ref[0, j]) + bias

    def finish(kv, j, s):
        t_sink = jnp.full((1, tq), sink_ref[kv * g + j] * _LOG2E, _F32)
        m = t_sink
        for part in s:
            m = jnp.maximum(m, jnp.max(part, axis=0, keepdims=True))
        vt_ctx = jnp.concatenate([vc_ref[0, cb, vrows(kv), :] for cb in range(nctx)], axis=1)
        acc = _dot(vt_ctx, jnp.exp2(s[0] - m).astype(_BF))
        if has_lat:
            sb = start // LANES
            vt_lat = jnp.concatenate([vl_ref[0, sb + cb, vrows(kv), :] for cb in range(band // LANES)], axis=1)
            acc = acc + _dot(vt_lat, jnp.exp2(s[1] - m).astype(_BF))
        den = acc[WIN_HEAD_DIM:WIN_HEAD_DIM + 1] + jnp.exp2(t_sink - m)
        return acc[:WIN_HEAD_DIM] / den

    units = [(kv, j) for j in range(g) for kv in range(WIN_KV_HEADS)]
    s = {u: scores(*units[u]) for u in range(min(_WIN_AHEAD, len(units)))}
    outs = {}
    for u, (kv, j) in enumerate(units):
        nxt = u + _WIN_AHEAD
        if nxt < len(units):
            s[nxt] = scores(*units[nxt])
        outs[(kv, j)] = finish(kv, j, s.pop(u))
        if kv == WIN_KV_HEADS - 1:
            both = jnp.concatenate([outs.pop((k2, j)) for k2 in range(WIN_KV_HEADS)], axis=0)
            o_ref[0, :, j * LANES:(j + 1) * LANES] = jnp.transpose(both).astype(_BF)


def _win_attention(q, k_lat, vt_lat, k_ctx, vt_ctx, sink, has_lat):
    b, _, t, _ = q.shape
    lc = k_ctx.shape[1]
    tq = 256 if has_lat else t
    tl = k_lat.shape[1]
    return pl.pallas_call(
        functools.partial(_win_kernel, t=t, tq=tq, has_lat=has_lat),
        grid=(b, t // tq),
        in_specs=[pl.BlockSpec(memory_space=pltpu.SMEM),
                  pl.BlockSpec((1, 8, tq, LANES), lambda b, n: (b, 0, n, 0)),
                  pl.BlockSpec((1, tl, 256), lambda b, n: (b, 0, 0)),
                  pl.BlockSpec((1,) + vt_lat.shape[1:], lambda b, n: (b, 0, 0, 0)),
                  pl.BlockSpec((1, lc, 256), lambda b, n: (b, 0, 0)),
                  pl.BlockSpec((1,) + vt_ctx.shape[1:], lambda b, n: (b, 0, 0, 0))],
        out_specs=pl.BlockSpec((1, tq, 512), lambda b, n: (b, n, 0)),
        out_shape=jax.ShapeDtypeStruct((b, t, 512), _BF),
        compiler_params=_params("parallel", "parallel"),
        name="win_attention" if has_lat else "ctx_attention",
    )(sink, q, k_lat, vt_lat, k_ctx, vt_ctx)


def _out_kernel(a_ref, b_ref, wa_ref, wb_ref, x_ref, mod_ref, g_ref, xo_ref):
    half = a_ref.shape[1] // 2
    rows = [slice(r * half, (r + 1) * half) for r in range(2)]
    ys = [_dot(a_ref[0, r, :], wa_ref[...]) + _dot(b_ref[0, r, :], wb_ref[...]) for r in rows]
    for r, y in zip(rows, ys):
        xo_ref[0, r, :] = x_ref[0, r, :] + mod_ref[0, 2:3] * _rms(y, g_ref[1:2])


def _mixer_out(a, bb, wa, wb, x, mod, mod_row, g):
    b, t, d = x.shape
    tm = min(1024, t)
    return pl.pallas_call(
        _out_kernel,
        grid=(b, t // tm),
        in_specs=[pl.BlockSpec((1, tm, 512), lambda b, i: (b, i, 0)),
                  pl.BlockSpec((1, tm, 512), lambda b, i: (b, i, 0)),
                  _const_spec(wa.shape), _const_spec(wb.shape),
                  pl.BlockSpec((1, tm, d), lambda b, i: (b, i, 0)),
                  _mod_spec(mod_row),
                  _const_spec(g.shape)],
        out_specs=pl.BlockSpec((1, tm, d), lambda b, i: (b, i, 0)),
        out_shape=jax.ShapeDtypeStruct((b, t, d), _F32),
        compiler_params=_params("parallel", "parallel"),
        name="mixer_out",
    )(a, bb, wa, wb, x, mod, g)


def _ffn_kernel(x_ref, xp_ref, xn_ref, wup_ref, cw_ref, cb_ref, wdn_ref, mod_ref, g_ref,
                o_ref, gbuf_ref, acc_ref, *, tm, nt):
    i = pl.program_id(1)

    def pre_norm(v):
        return (_rms(v, g_ref[2:3]) * (1.0 + mod_ref[0, 4:5]) + mod_ref[0, 3:4]).astype(_BF)

    cur = pre_norm(x_ref[0])
    lhs = jnp.concatenate([pre_norm(xp_ref[0]), cur, pre_norm(xn_ref[0])], axis=0)
    edges = ((HALO - 1, i > 0), (HALO + tm, i < nt - 1))

    def cols(c):
        return slice(c * FFN_CHUNK, (c + 1) * FFN_CHUNK)

    def gate(c):
        return _dot(lhs, wup_ref[:, cols(c)])

    def activation(c, g):
        buf = gbuf_ref.at[c % 2]
        buf[...] = g
        for r, inside in edges:
            buf[r:r + 1, :] = jnp.where(inside, buf[r:r + 1, :], 0.0)
        w = cw_ref[:, cols(c)]
        gc = (w[0:1] * buf[HALO - 1:HALO - 1 + tm, :] + w[1:2] * buf[HALO:HALO + tm, :]
              + w[2:3] * buf[HALO + 1:HALO + 1 + tm, :] + cb_ref[:, cols(c)])
        u = _dot(cur, wup_ref[:, cols(FFN_NCHUNK + c)])
        return (jax.nn.gelu(gc) * u).astype(_BF)

    pending = gate(0)
    acts = []
    for c in range(FFN_NCHUNK):
        following = gate(c + 1) if c + 1 < FFN_NCHUNK else None
        acts.append(activation(c, pending))
        pending = following
        if len(acts) == _FFN_DOWN or c == FFN_NCHUNK - 1:
            c0 = c + 1 - len(acts)
            y = _dot(jnp.concatenate(acts, axis=1), wdn_ref[c0 * FFN_CHUNK:(c + 1) * FFN_CHUNK, :])
            if c0 == 0:
                acc_ref[...] = y
            else:
                acc_ref[...] += y
            acts = []
    o_ref[0] = x_ref[0] + mod_ref[0, 5:6] * _rms(acc_ref[...], g_ref[3:4])


def _ffn(x, wup, cw, cb, wdn, mod, mod_row, g):
    b, t, d = x.shape
    tm = min(512, t)
    nt = t // tm
    hb = tm // HALO
    return pl.pallas_call(
        functools.partial(_ffn_kernel, tm=tm, nt=nt),
        grid=(b, nt),
        in_specs=[pl.BlockSpec((1, tm, d), lambda b, i: (b, i, 0)),
                  pl.BlockSpec((1, HALO, d), lambda b, i: (b, jnp.maximum(i * hb - 1, 0), 0)),
                  pl.BlockSpec((1, HALO, d), lambda b, i: (b, jnp.minimum((i + 1) * hb, t // HALO - 1), 0)),
                  _const_spec(wup.shape), _const_spec(cw.shape), _const_spec(cb.shape), _const_spec(wdn.shape),
                  _mod_spec(mod_row),
                  _const_spec(g.shape)],
        out_specs=pl.BlockSpec((1, tm, d), lambda b, i: (b, i, 0)),
        out_shape=jax.ShapeDtypeStruct((b, t, d), _F32),
        scratch_shapes=[pltpu.VMEM((2, tm + 2 * HALO, FFN_CHUNK), _F32),
                        pltpu.VMEM((tm, d), _F32)],
        compiler_params=_params("parallel", "parallel"),
        name="conv_glu_ffn",
    )(x, x, x, wup, cw, cb, wdn, mod, g)


_CD_CQ0, _CD_CKV0, _CD_KR0, _CD_N = 1024, 1408, 1664, 1792


def _cd_in_kernel(x_ref, mod_ref, g_ref, w1_ref, qn_ref, wq_ref, kvn_ref, wkv_ref,
                  qt_ref, kt_ref, xg_ref, q_ref, k_ref, v_ref):
    half = x_ref.shape[1] // 2
    rows = [slice(r * half, (r + 1) * half) for r in range(2)]
    zs = [_dot((_rms(x_ref[0, r, :], g_ref[0:1]) * (1.0 + mod_ref[0, 1:2]) + mod_ref[0, 0:1]).astype(_BF), w1_ref[...])
          for r in rows]
    ups = [(_dot(_rms(z[:, _CD_CQ0:_CD_CKV0], qn_ref[...]).astype(_BF), wq_ref[...]),
            _dot(_rms(z[:, _CD_CKV0:_CD_KR0], kvn_ref[...]).astype(_BF), wkv_ref[...])) for z in zs]
    quarter = MLA_ROPE // 4
    nq = MLA_HEADS * LANES
    ones = jnp.ones((_MLA_VT - MLA_V, half), _BF)
    for r, z, (qq, kv) in zip(rows, zs, ups):
        xg_ref[0, r, :] = z[:, :_CD_CQ0]
        qt = [qt_ref[i, r, :] for i in range(3)]
        for hd in range(MLA_HEADS):
            lo, hi = hd * LANES, (hd + 1) * LANES
            q_ref[0, r, lo:hi] = _rope(qq[:, lo:hi], *qt, quarter).astype(_BF)
        krr = _rope(z[:, _CD_KR0:_CD_N], *[kt_ref[i, r, :] for i in range(3)], quarter)
        for hd in range(MLA_HEADS):
            lo, hi = hd * LANES, (hd + 1) * LANES
            k_ref[0, r, lo:hi] = (kv[:, lo:hi] + krr).astype(_BF)
        vt = jnp.transpose(kv[:, nq:]).astype(_BF)
        for hd in range(MLA_HEADS):
            v_ref[0, 0, hd * _MLA_VT:hd * _MLA_VT + MLA_V, r] = vt[hd * MLA_V:(hd + 1) * MLA_V]
            v_ref[0, 0, hd * _MLA_VT + MLA_V:(hd + 1) * _MLA_VT, r] = ones


def _cd_in(x, mod, mod_row, g, w1, qn, wq, kvn, wkv, qtab, ktab):
    b, t, d = x.shape
    tm = min(512, t)
    tab = pl.BlockSpec((3, tm, LANES), lambda b, i: (0, i, 0))
    return pl.pallas_call(
        _cd_in_kernel,
        grid=(b, t // tm),
        in_specs=[pl.BlockSpec((1, tm, d), lambda b, i: (b, i, 0)),
                  _mod_spec(mod_row),
                  _const_spec(g.shape), _const_spec(w1.shape), _const_spec(qn.shape), _const_spec(wq.shape),
                  _const_spec(kvn.shape), _const_spec(wkv.shape), tab, tab],
        out_specs=[pl.BlockSpec((1, tm, 1024), lambda b, i: (b, i, 0)),
                   pl.BlockSpec((1, tm, 1024), lambda b, i: (b, i, 0)),
                   pl.BlockSpec((1, tm, 1024), lambda b, i: (b, i, 0)),
                   pl.BlockSpec((1, 1, MLA_HEADS * _MLA_VT, tm), lambda b, i: (b, i, 0, 0))],
        out_shape=[jax.ShapeDtypeStruct((b, t, 1024), _F32),
                   jax.ShapeDtypeStruct((b, t, 1024), _BF),
                   jax.ShapeDtypeStruct((b, t, 1024), _BF),
                   jax.ShapeDtypeStruct((b, t // tm, MLA_HEADS * _MLA_VT, tm), _BF)],
        compiler_params=_params("parallel", "parallel"),
        name="cd_in",
    )(x, mod, g, w1, qn, wq, kvn, wkv, qtab, ktab)


def _lru_kernel(*refs, tt, reverse, combine):
    if combine:
        (x_ref, h0_ref, cw_ref, cb_ref, wg_ref, gb_ref, lam_ref, hb_ref, gt_ref,
         o_ref, xext_ref, a_ref, b_ref, hcar_ref, xcar_ref, hs_ref) = refs
    else:
        (x_ref, h0_ref, cw_ref, cb_ref, wg_ref, gb_ref, lam_ref,
         o_ref, xext_ref, a_ref, b_ref, hcar_ref, xcar_ref) = refs
        hs_ref = o_ref.at[0]
    i = pl.program_id(1)

    @pl.when(i == 0)
    def _():
        hcar_ref[...] = jnp.broadcast_to(h0_ref[0], (_SLAB, LRU_DIM))
        xcar_ref[...] = jnp.zeros_like(xcar_ref)

    x = x_ref[0]
    if reverse:
        xext_ref[0:tt] = x
        xext_ref[tt:tt + _SLAB] = xcar_ref[...]
        xcar_ref[...] = x[0:_SLAB]
        base = 0
    else:
        xext_ref[0:_SLAB] = xcar_ref[...]
        xext_ref[_SLAB:_SLAB + tt] = x
        xcar_ref[...] = x[tt - _SLAB:tt]
        base = _SLAB - (LRU_CONV_WIDTH - 1)
    xc = cb_ref[...]
    for j in range(LRU_CONV_WIDTH):
        xc = xc + cw_ref[j:j + 1, :] * xext_ref[base + j:base + j + tt, :]
    gates = _dot(xc.astype(_BF), wg_ref[...]) + gb_ref[...]
    r = jax.nn.sigmoid(gates[:, :LRU_DIM])
    ig = jax.nn.sigmoid(gates[:, LRU_DIM:])
    nl = -lam_ref[...]
    neg_sp = jnp.maximum(nl, 0.0) + jnp.log1p(jnp.exp(-jnp.abs(nl)))
    log_a = -LRU_C * r * neg_sp
    th = jnp.tanh(log_a)
    a_ref[...] = jnp.exp(log_a)
    y = -2.0 * th / (1.0 - th)
    b_ref[...] = jnp.where(y > 0.0, y * lax.rsqrt(y), 0.0) * ig * xc

    ns = tt // _SLAB
    rowi = lax.broadcasted_iota(jnp.int32, (_SLAB, LRU_DIM), 0)

    def slab(s, h):
        r0 = pl.multiple_of(((ns - 1 - s) if reverse else s) * _SLAB, _SLAB)
        av = a_ref[pl.ds(r0, _SLAB), :]
        bv = b_ref[pl.ds(r0, _SLAB), :]
        for sh in (1, 2, 4):
            if reverse:
                ok = rowi < _SLAB - sh
                a_sh = pltpu.roll(av, _SLAB - sh, 0)
                b_sh = pltpu.roll(bv, _SLAB - sh, 0)
            else:
                ok = rowi >= sh
                a_sh = pltpu.roll(av, sh, 0)
                b_sh = pltpu.roll(bv, sh, 0)
            bv = jnp.where(ok, av * b_sh + bv, bv)
            av = jnp.where(ok, av * a_sh, av)
        hs = av * h + bv
        hs_ref[pl.ds(r0, _SLAB), :] = hs
        last = hs[0:1] if reverse else hs[_SLAB - 1:_SLAB]
        return jnp.broadcast_to(last, (_SLAB, LRU_DIM))

    hcar_ref[...] = lax.fori_loop(0, ns, slab, hcar_ref[...])
    if combine:
        o_ref[0] = ((hs_ref[...] + hb_ref[0]) * jax.nn.gelu(gt_ref[0])).astype(o_ref.dtype)


def _rglru(xg, h0, conv_w, conv_b, wg, gate_b, lam, reverse, hb=None):
    b, t, _ = xg.shape
    tt = min(512, t)
    nt = t // tt
    combine = hb is not None
    tidx = (lambda i: nt - 1 - i) if reverse else (lambda i: i)
    in_specs = [pl.BlockSpec((1, tt, LRU_DIM), lambda b, i: (b, tidx(i), 0)),
                pl.BlockSpec((1, 1, LRU_DIM), lambda b, i: (b, 0, 0)),
                _const_spec(conv_w.shape), _const_spec((1, LRU_DIM)), _const_spec(wg.shape),
                _const_spec((1, 2 * LRU_DIM)), _const_spec((1, LRU_DIM))]
    args = [xg, h0.reshape(b, 1, LRU_DIM), conv_w, conv_b.reshape(1, -1), wg, gate_b.reshape(1, -1), lam.reshape(1, -1)]
    if combine:
        in_specs += [pl.BlockSpec((1, tt, LRU_DIM), lambda b, i: (b, tidx(i), 0)),
                     pl.BlockSpec((1, tt, LRU_DIM), lambda b, i: (b, tidx(i), 1))]
        args += [hb, xg]
    return pl.pallas_call(
        functools.partial(_lru_kernel, tt=tt, reverse=reverse, combine=combine),
        grid=(b, nt),
        in_specs=in_specs,
        out_specs=pl.BlockSpec((1, tt, LRU_DIM), lambda b, i: (b, tidx(i), 0)),
        out_shape=jax.ShapeDtypeStruct((b, t, LRU_DIM), _BF if combine else _F32),
        scratch_shapes=[pltpu.VMEM((tt + _SLAB, LRU_DIM), _F32),
                        pltpu.VMEM((tt, LRU_DIM), _F32),
                        pltpu.VMEM((tt, LRU_DIM), _F32),
                        pltpu.VMEM((_SLAB, LRU_DIM), _F32),
                        pltpu.VMEM((_SLAB, LRU_DIM), _F32)]
        + ([pltpu.VMEM((tt, LRU_DIM), _F32)] if combine else []),
        compiler_params=_params("parallel", "arbitrary"),
        name="rglru_combine" if combine else ("rglru_rev" if reverse else "rglru_fwd"),
    )(*args)


def _mla_kernel(q_ref, kl_ref, vl_ref, kc_ref, vc_ref, o_ref, m_ref, acc_ref, *, nchunk, tq):
    m_ref[...] = jnp.full(m_ref.shape, NEG_INF, _F32)
    acc_ref[...] = jnp.zeros(acc_ref.shape, _F32)

    def lanes(hd):
        return slice(hd * LANES, (hd + 1) * LANES)

    def vrows(hd):
        return slice(hd * _MLA_VT, (hd + 1) * _MLA_VT)

    def scores(hd, k):
        return _dot_nt(k, q_ref[0, :, lanes(hd)])

    def update(hd, s, vt):
        m = m_ref[hd]
        m_new = jnp.maximum(m, jnp.max(s, axis=0, keepdims=True))
        p = jnp.exp2(s - m_new).astype(_BF)
        acc_ref[hd] = acc_ref[hd] * jnp.exp2(m - m_new) + _dot(vt, p)
        m_ref[hd] = m_new

    def run(units):
        s = {j: scores(units[j][0], units[j][1]()) for j in range(min(_MLA_AHEAD, len(units)))}
        for j, (hd, _, values) in enumerate(units):
            nxt = j + _MLA_AHEAD
            if nxt < len(units):
                s[nxt] = scores(units[nxt][0], units[nxt][1]())
            update(hd, s.pop(j), values())

    unroll = max(u for u in range(1, _MLA_UNROLL + 1) if nchunk % u == 0)

    def latent_units(it):
        units = []
        for sub in range(unroll):
            ci = it * unroll + sub
            r0 = ci * _MLA_TK if isinstance(ci, int) else pl.multiple_of(ci * _MLA_TK, _MLA_TK)
            for c0 in range(0, _MLA_TK, _MLA_SUB):
                units += [(hd, functools.partial(lambda hd, r: kl_ref[0, pl.ds(r, _MLA_SUB), lanes(hd)], hd, r0 + c0),
                           functools.partial(lambda hd, ci, c0: vl_ref[0, ci, vrows(hd), c0:c0 + _MLA_SUB], hd, ci, c0))
                          for hd in range(MLA_HEADS)]
        return units

    context_units = [(hd, functools.partial(lambda hd: kc_ref[0, :, lanes(hd)], hd),
                      functools.partial(lambda hd: vc_ref[0, 0, vrows(hd), :], hd)) for hd in range(MLA_HEADS)]
    niter = nchunk // unroll

    def body(it, carry):
        run(latent_units(it))
        return carry

    lax.fori_loop(0, niter - 1, body, 0)
    run(latent_units(niter - 1) + context_units)
    for pair in range(MLA_HEADS // 2):
        outs = [acc_ref[hd, :MLA_V] / acc_ref[hd, MLA_V:MLA_V + 1] for hd in (2 * pair, 2 * pair + 1)]
        o_ref[0, :, lanes(pair)] = jnp.transpose(jnp.concatenate(outs, axis=0)).astype(_BF)


def _mla_attention(q, k_lat, vt_lat, k_ctx, vt_ctx):
    b, t, _ = q.shape
    lc = k_ctx.shape[1]
    nchunk = vt_lat.shape[1]
    tq = 256
    return pl.pallas_call(
        functools.partial(_mla_kernel, nchunk=nchunk, tq=tq),
        grid=(b, t // tq),
        in_specs=[pl.BlockSpec((1, tq, 1024), lambda b, i: (b, i, 0)),
                  pl.BlockSpec((1, t, 1024), lambda b, i: (b, 0, 0)),
                  pl.BlockSpec((1,) + vt_lat.shape[1:], lambda b, i: (b, 0, 0, 0)),
                  pl.BlockSpec((1, lc, 1024), lambda b, i: (b, 0, 0)),
                  pl.BlockSpec((1,) + vt_ctx.shape[1:], lambda b, i: (b, 0, 0, 0))],
        out_specs=pl.BlockSpec((1, tq, 512), lambda b, i: (b, i, 0)),
        out_shape=jax.ShapeDtypeStruct((b, t, 512), _BF),
        scratch_shapes=[pltpu.VMEM((MLA_HEADS, 1, tq), _F32),
                        pltpu.VMEM((MLA_HEADS, _MLA_VT, tq), _F32)],
        compiler_params=_params("parallel", "parallel"),
        name="mla_attention",
    )(q, k_lat, vt_lat, k_ctx, vt_ctx)


def _rope_tables(t, dim):
    rows = t // GRID_W
    row = np.repeat(np.arange(rows), GRID_W).astype(np.float32)
    col = np.tile(np.arange(GRID_W), rows).astype(np.float32)
    nf = dim // 4
    inv = (np.float32(ROPE_THETA) ** (-np.arange(nf, dtype=np.float32) / np.float32(nf))).astype(np.float32)
    ar = row[:, None] * inv
    ac = col[:, None] * inv
    ang = np.concatenate([ar, ar, ac, ac], axis=-1).astype(np.float32)
    return np.cos(ang).astype(np.float32), np.sin(ang).astype(np.float32)


def _rope_slot_tables(cos, sin, lo, scale=1.0):
    t, dim = cos.shape
    q = dim // 4
    sign = np.concatenate([-np.ones(q), np.zeros(q), -np.ones(q), np.zeros(q)]).astype(np.float32)
    out = np.zeros((3, t, LANES), np.float32)
    out[0] = 1.0
    for l0 in (lo if isinstance(lo, (tuple, list)) else (lo,)):
        out[0, :, l0:l0 + dim] = cos
        out[1, :, l0:l0 + dim] = sin * sign
        out[2, :, l0:l0 + dim] = sin * (1.0 + sign)
    return out * np.float32(scale)


def _ab_weights(w_in, w_out):
    hd = WIN_HEAD_DIM
    g = WIN_HEADS // WIN_KV_HEADS
    glu = w_in[:, :1024]
    wq = w_in[:, 1024:1536]
    wk = w_in[:, 1536:1664]
    wv = w_in[:, 1664:1792]
    zeros = jnp.zeros_like(wk[:, :hd])
    qh = [wq[:, h * hd:(h + 1) * hd] for h in range(WIN_HEADS)]
    kh = [wk[:, h * hd:(h + 1) * hd] for h in range(WIN_KV_HEADS)]
    q_full = jnp.concatenate([jnp.concatenate([qh[j], qh[g + j]], 1) for j in range(g)], 1)
    k_full = jnp.concatenate([kh[0], zeros, zeros, kh[1]], 1)
    w_ext = jnp.concatenate([glu, q_full, k_full, wv], 1).astype(_BF)
    wa = w_out[:CONV_A_DIM].astype(_BF)
    wb_rows = w_out[CONV_A_DIM:]
    wb = jnp.concatenate([jnp.concatenate([wb_rows[j * hd:(j + 1) * hd], wb_rows[(g + j) * hd:(g + j + 1) * hd]], 0)
                          for j in range(g)], 0).astype(_BF)
    return w_ext, wa, wb


def _cd_weights(w_in, w_uq, w_ukv, lat):
    d = w_in.shape[0]
    kr = w_in[:, 1664:1696]
    z32 = jnp.zeros((d, MLA_ROPE), w_in.dtype)
    z64 = jnp.zeros((d, MLA_NOPE), w_in.dtype)
    kr128 = jnp.concatenate([z64, kr, z32] if lat else [z64, z32, kr], 1)
    w1 = jnp.concatenate([w_in[:, :1664], kr128], 1).astype(_BF)
    qd = MLA_NOPE + MLA_ROPE
    full = []
    for h in range(MLA_HEADS):
        nope = w_uq[:, h * qd:h * qd + MLA_NOPE]
        rope = w_uq[:, h * qd + MLA_NOPE:(h + 1) * qd]
        full += [nope, rope, rope]
    wq = jnp.concatenate(full, 1).astype(_BF)
    kd = MLA_NOPE + MLA_V
    zk = jnp.zeros((MLA_KV_RANK, LANES - MLA_NOPE), w_ukv.dtype)
    kcols, vcols = [], []
    for h in range(MLA_HEADS):
        kcols += [w_ukv[:, h * kd:h * kd + MLA_NOPE], zk]
        vcols += [w_ukv[:, h * kd + MLA_NOPE:(h + 1) * kd]]
    wkv = jnp.concatenate(kcols + vcols, 1).astype(_BF)
    return w1, wq, wkv


def _ffn_weights(w_up, conv_w, conv_b, w_down):
    return w_up.astype(_BF), conv_w, conv_b.reshape(1, -1), w_down.astype(_BF)


def _block_diag_dense(w):
    nb, hh, kk = w.shape
    eye = jnp.eye(nb, dtype=w.dtype)
    return (eye[:, None, :, None] * w[:, :, None, :]).reshape(nb * hh, nb * kk)


def kernel(x, c, ctx, c_ctx, w_mod, b_mod, norm_g, ffn_w_up, ffn_conv_w, ffn_conv_b, ffn_w_down, ab_w_in, a_conv_w, a_conv_b, a_ln_g, a_ln_b, b_sink, ab_w_out, cd_w_in, lru_conv_w, lru_conv_b, lru_gate_w, lru_gate_b, lru_lambda, mla_q_norm, mla_w_uq, mla_kv_norm, mla_w_ukv, cd_w_out):
    b, t, d = x.shape
    lc = ctx.shape[1]
    ctx_row = 8 * ((b + 7) // 8)
    cc = jnp.zeros((ctx_row + 8, d), _F32).at[:b].set(c).at[ctx_row].set(c_ctx)
    mods = _modulation(cc, w_mod, b_mod).reshape(w_mod.shape[0], ctx_row + 8, N_MOD, d)

    mod, g = mods[0], norm_g[0]
    w_ext, wa, wb = _ab_weights(ab_w_in[0], ab_w_out[0])
    cos64, sin64 = _rope_tables(t, WIN_HEAD_DIM)
    halves = (0, WIN_HEAD_DIM)
    ident = _rope_slot_tables(np.ones((lc, 4), np.float32), np.zeros((lc, 4), np.float32), 0)
    glu_l, q_l, k_l, v_l = _ab_in(x, mod, None, g, w_ext, _rope_slot_tables(cos64, sin64, halves, _WIN_QSCALE),
                                  _rope_slot_tables(cos64, sin64, halves))
    glu_c, q_c, k_c, v_c = _ab_in(ctx, mod, ctx_row, g, w_ext, ident * np.float32(_WIN_QSCALE), ident)
    a_l = _conformer_conv(glu_l, a_conv_w[0], a_conv_b[0], a_ln_g[0], a_ln_b[0])
    a_c = _conformer_conv(glu_c, a_conv_w[0], a_conv_b[0], a_ln_g[0], a_ln_b[0])
    b_l = _win_attention(q_l, k_l, v_l, k_c, v_c, b_sink[0], True)
    b_c = _win_attention(q_c, k_c, v_c, k_c, v_c, b_sink[0], False)
    xl = _mixer_out(a_l, b_l, wa, wb, x, mod, None, g)
    xc = _mixer_out(a_c, b_c, wa, wb, ctx, mod, ctx_row, g)
    fw = _ffn_weights(ffn_w_up[0], ffn_conv_w[0], ffn_conv_b[0], ffn_w_down[0])
    xl = _ffn(xl, *fw, mod, None, g)
    xc = _ffn(xc, *fw, mod, ctx_row, g)

    mod, g = mods[1], norm_g[1]
    w1_l, wq, wkv = _cd_weights(cd_w_in[0], mla_w_uq[0], mla_w_ukv[0], True)
    w1_c, _, _ = _cd_weights(cd_w_in[0], mla_w_uq[0], mla_w_ukv[0], False)
    cos32, sin32 = _rope_tables(t, MLA_ROPE)
    qn = mla_q_norm[0].reshape(1, -1)
    kvn = mla_kv_norm[0].reshape(1, -1)
    qs = np.float32((MLA_NOPE + MLA_ROPE) ** -0.5 * _LOG2E)
    xg_l, q_l, k_l, v_l = _cd_in(xl, mod, None, g, w1_l, qn, wq, kvn, wkv, _rope_slot_tables(cos32, sin32, MLA_NOPE, qs),
                                 _rope_slot_tables(cos32, sin32, MLA_NOPE))
    xg_c, _, k_c, v_c = _cd_in(xc, mod, ctx_row, g, w1_c, qn, wq, kvn, wkv, ident, ident)
    wg = [jnp.concatenate([_block_diag_dense(lru_gate_w[0, dr, 0]), _block_diag_dense(lru_gate_w[0, dr, 1])], 1).astype(_BF)
          for dr in range(2)]
    gb = [jnp.concatenate([lru_gate_b[0, dr, 0], lru_gate_b[0, dr, 1]]) for dr in range(2)]
    lru = lambda xg, h0, dr, hb=None: _rglru(xg, h0, lru_conv_w[0, dr], lru_conv_b[0, dr], wg[dr], gb[dr],
                                             lru_lambda[0, dr], dr == 1, hb)
    h_zero = jnp.zeros((b, LRU_DIM), _F32)
    hf_c = lru(xg_c, h_zero, 0)
    hb_c = lru(xg_c, h_zero, 1)
    hb_l = lru(xg_l, hb_c[:, 0], 1)
    c_l = lru(xg_l, hf_c[:, lc - 1], 0, hb_l)
    d_l = _mla_attention(q_l, k_l, v_l, k_c, v_c)
    wo = cd_w_out[0]
    xl = _mixer_out(c_l, d_l, wo[:LRU_DIM].astype(_BF), wo[LRU_DIM:].astype(_BF), xl, mod, None, g)
    fw = _ffn_weights(ffn_w_up[1], ffn_conv_w[1], ffn_conv_b[1], ffn_w_down[1])
    return _ffn(xl, *fw, mod, None, g)
```
